```python
import jax
import jax.numpy as jnp
from jax import lax
import numpy as np

D_MODEL = 1024
BATCH = 8
SEQ = 8192
DEPTH = 1

ATT_HEADS = 8
ATT_KV_HEADS = 2
ATT_HEAD_DIM = 64
WINDOW = 128
ATT_BLOCK = 128
RET_HEADS = 4
RET_QK_DIM = 64
RET_V_DIM = 128
RET_CHUNK = 128
MEM_LEN = 256
MEM_HEADS = 4
MEM_HEAD_DIM = 128
PEER_HEADS = 8
PEER_N_KEYS = 128
PEER_N_EXPERTS = PEER_N_KEYS * PEER_N_KEYS
PEER_TOPK = 16
PEER_QUERY_DIM = 256
PEER_HALF = PEER_QUERY_DIM // 2
PEER_TOKEN_CHUNK = 128
N_BRANCHES = 3
ALPHA = (2.0 * DEPTH) ** 0.25
BETA = (8.0 * DEPTH) ** -0.25
LN_EPS = 1e-5

ATT_Q_WIDTH = ATT_HEADS * ATT_HEAD_DIM
ATT_KV_WIDTH = ATT_KV_HEADS * ATT_HEAD_DIM
RET_QK_WIDTH = RET_HEADS * RET_QK_DIM
RET_V_WIDTH = RET_HEADS * RET_V_DIM
MEM_WIDTH = MEM_HEADS * MEM_HEAD_DIM
GATE_WIDTH = N_BRANCHES * D_MODEL
IN_SIZES = (ATT_Q_WIDTH, ATT_KV_WIDTH, ATT_KV_WIDTH, RET_QK_WIDTH, RET_QK_WIDTH, RET_V_WIDTH, RET_V_WIDTH, MEM_WIDTH, GATE_WIDTH)
IN_WIDTH = sum(IN_SIZES)

kernel_name = "hybrid_swa_retention_mem_peer_deepnorm"

F32 = jnp.float32


def layer_norm(x, g, b):
    xf = x.astype(F32)
    mu = jnp.mean(xf, axis=-1, keepdims=True)
    var = jnp.mean(jnp.square(xf - mu), axis=-1, keepdims=True)
    y = (xf - mu) * lax.rsqrt(var + LN_EPS) * g.astype(F32) + b.astype(F32)
    return y.astype(x.dtype)


def sliding_window_gqa(q, k, v, sinks):
    b, s = q.shape[0], q.shape[1]
    nb = s // ATT_BLOCK
    rep = ATT_HEADS // ATT_KV_HEADS
    qb = q.reshape(b, nb, ATT_BLOCK, ATT_KV_HEADS, rep, ATT_HEAD_DIM)

    def band(t):
        tb = t.reshape(b, nb, ATT_BLOCK, ATT_KV_HEADS, ATT_HEAD_DIM)
        prev = jnp.pad(tb[:, :-1], ((0, 0), (1, 0), (0, 0), (0, 0), (0, 0)))
        return jnp.concatenate([prev, tb], axis=2)

    kb, vb = band(k), band(v)
    logits = jnp.einsum('bnqgrd,bnkgd->bngrqk', qb, kb).astype(F32) * (ATT_HEAD_DIM ** -0.5)
    qi = jnp.arange(ATT_BLOCK)[:, None]
    kj = jnp.arange(2 * ATT_BLOCK)[None, :]
    dist = qi + ATT_BLOCK - kj
    in_window = (dist >= 0) & (dist < WINDOW)
    has_prev = jnp.arange(nb)[:, None, None] > 0
    mask = in_window[None] & (has_prev | (kj >= ATT_BLOCK)[None])
    logits = jnp.where(mask[None, :, None, None], logits, -jnp.inf)
    sink = sinks.astype(F32).reshape(ATT_KV_HEADS, rep)[None, None, :, :, None, None]
    m = jnp.maximum(jnp.max(logits, axis=-1, keepdims=True), sink)
    p = jnp.exp(logits - m)
    p = p / (jnp.sum(p, axis=-1, keepdims=True) + jnp.exp(sink - m))
    out = jnp.einsum('bngrqk,bnkgd->bnqgrd', p.astype(v.dtype), vb)
    return out.reshape(b, s, ATT_Q_WIDTH)


def retention_rotate(t, pos):
    half = RET_QK_DIM // 2
    theta = 1.0 / (10000.0 ** jnp.linspace(0.0, 1.0, half, dtype=F32))
    ang = pos[:, None] * theta[None, :]
    cos = jnp.cos(ang)[None, :, None, :]
    sin = jnp.sin(ang)[None, :, None, :]
    t1, t2 = t[..., :half], t[..., half:]
    return jnp.concatenate([t1 * cos - t2 * sin, t1 * sin + t2 * cos], axis=-1)


def chunkwise_retention(q, k, v):
    b, s = q.shape[0], q.shape[1]
    nc = s // RET_CHUNK
    log_gamma = jnp.log(1.0 - 2.0 ** (-5.0 - jnp.arange(RET_HEADS, dtype=F32)))
    k = k * (RET_QK_DIM ** -0.5)
    qc = q.reshape(b, nc, RET_CHUNK, RET_HEADS, RET_QK_DIM)
    kc = k.reshape(b, nc, RET_CHUNK, RET_HEADS, RET_QK_DIM)
    vc = v.reshape(b, nc, RET_CHUNK, RET_HEADS, RET_V_DIM)
    i = jnp.arange(RET_CHUNK, dtype=F32)
    diff = i[:, None] - i[None, :]
    decay_in = jnp.where(diff >= 0, jnp.exp(log_gamma[:, None, None] * jnp.maximum(diff, 0.0)), 0.0)
    inner = jnp.einsum('bnihd,bnjhd->bnhij', qc, kc) * decay_in
    o_in = jnp.einsum('bnhij,bnjhe->bnihe', inner, vc)
    w_k = jnp.exp(log_gamma[:, None] * (RET_CHUNK - 1.0 - i)[None, :])
    kv = jnp.einsum('bnjhd,bnjhe,hj->nbhde', kc, vc, w_k)
    chunk_decay = jnp.exp(log_gamma * RET_CHUNK)[None, :, None, None]

    def step(state, kv_n):
        return chunk_decay * state + kv_n, state

    init = jnp.zeros((b, RET_HEADS, RET_QK_DIM, RET_V_DIM), F32)
    _, states = lax.scan(step, init, kv)
    w_q = jnp.exp(log_gamma[:, None] * (i + 1.0)[None, :])
    o_x = jnp.einsum('bnihd,nbhde,hi->bnihe', qc, states, w_q)
    return (o_in + o_x).reshape(b, s, RET_HEADS, RET_V_DIM)


def head_group_norm(t, g):
    mu = jnp.mean(t, axis=-1, keepdims=True)
    var = jnp.mean(jnp.square(t - mu), axis=-1, keepdims=True)
    return (t - mu) * lax.rsqrt(var + LN_EPS) * g.astype(F32).reshape(RET_HEADS, RET_V_DIM)


def memory_cross_attention(q, mem, mem_ln_g, mem_ln_b, w_mem_kv):
    b, s = q.shape[0], q.shape[1]
    m_len = mem.shape[1]
    mkv = layer_norm(mem, mem_ln_g, mem_ln_b) @ w_mem_kv
    mk, mv = jnp.split(mkv, 2, axis=-1)
    mk = mk.reshape(b, m_len, MEM_HEADS, MEM_HEAD_DIM)
    mv = mv.reshape(b, m_len, MEM_HEADS, MEM_HEAD_DIM)
    qh = q.reshape(b, s, MEM_HEADS, MEM_HEAD_DIM)
    logits = jnp.einsum('bshd,bmhd->bhsm', qh, mk).astype(F32) * (MEM_HEAD_DIM ** -0.5)
    p = jax.nn.softmax(logits, axis=-1).astype(mv.dtype)
    return jnp.einsum('bhsm,bmhd->bshd', p, mv).reshape(b, s, MEM_WIDTH)


def peer(h, w_peer_q, peer_sub_keys, peer_u, peer_v):
    b, s, d = h.shape
    n_chunks = (b * s) // PEER_TOKEN_CHUNK
    hc = h.reshape(n_chunks, PEER_TOKEN_CHUNK, d)

    def chunk(ht):
        t = ht.shape[0]
        q = (ht @ w_peer_q).reshape(t, PEER_HEADS, 2, PEER_HALF)
        sc = jnp.einsum('thpd,hpnd->thpn', q, peer_sub_keys).astype(F32)
        sv, si = lax.top_k(sc, PEER_TOPK)
        cand = (sv[:, :, 0, :, None] + sv[:, :, 1, None, :]).reshape(t, PEER_HEADS, PEER_TOPK * PEER_TOPK)
        cidx = (si[:, :, 0, :, None] * PEER_N_KEYS + si[:, :, 1, None, :]).reshape(t, PEER_HEADS, PEER_TOPK * PEER_TOPK)
        top_s, pos = lax.top_k(cand, PEER_TOPK)
        eidx = jnp.take_along_axis(cidx, pos, axis=-1)
        gate = jax.nn.softmax(top_s, axis=-1).astype(ht.dtype)
        u = peer_u[eidx]
        act = jax.nn.gelu(jnp.einsum('thkd,td->thk', u, ht), approximate=False)
        return jnp.einsum('thk,thkd->td', gate * act, peer_v[eidx])

    out = lax.map(chunk, hc)
    return out.reshape(b, s, d)


def hybrid_layer(x, mem, w_in, b_in, attn_sinks, ret_gn_g, mem_ln_g, mem_ln_b, w_mem_kv,
                 w_branch_attn, w_branch_ret, w_branch_mem, w_out, ln1_g, ln1_b,
                 w_peer_q, peer_sub_keys, peer_u, peer_v, ln2_g, ln2_b):
    b, s, d = x.shape
    pos = jnp.arange(s, dtype=F32)
    proj = x @ w_in + b_in
    offsets = np.cumsum(IN_SIZES)[:-1].tolist()
    aq, ak, av, rq, rk, rv, rg, mq, gates = jnp.split(proj, offsets, axis=-1)

    br_a = sliding_window_gqa(aq.reshape(b, s, ATT_HEADS, ATT_HEAD_DIM),
                              ak.reshape(b, s, ATT_KV_HEADS, ATT_HEAD_DIM),
                              av.reshape(b, s, ATT_KV_HEADS, ATT_HEAD_DIM), attn_sinks)

    rq = retention_rotate(rq.reshape(b, s, RET_HEADS, RET_QK_DIM).astype(F32), pos)
    rk = retention_rotate(rk.reshape(b, s, RET_HEADS, RET_QK_DIM).astype(F32), pos)
    ret = chunkwise_retention(rq, rk, rv.reshape(b, s, RET_HEADS, RET_V_DIM).astype(F32))
    ret = head_group_norm(ret, ret_gn_g).reshape(b, s, RET_V_WIDTH).astype(x.dtype)
    br_b = ret * jax.nn.silu(rg)

    br_c = memory_cross_attention(mq, mem, mem_ln_g, mem_ln_b, w_mem_kv)

    g = jax.nn.sigmoid(gates).reshape(b, s, N_BRANCHES, d)
    merged = (g[:, :, 0] * (br_a @ w_branch_attn) + g[:, :, 1] * (br_b @ w_branch_ret)
              + g[:, :, 2] * (br_c @ w_branch_mem))
    h = layer_norm(ALPHA * x + merged @ w_out, ln1_g, ln1_b)

    y = layer_norm(ALPHA * h + peer(h, w_peer_q, peer_sub_keys, peer_u, peer_v), ln2_g, ln2_b)
    return y


def setup_inputs(seed: int = 0) -> dict:
    key = jax.random.key(seed)
    ks = jax.random.split(key, 24)

    def nrm(k, shape, scale):
        return jax.random.normal(k, shape, F32) * scale

    L = DEPTH
    return {
        "x": nrm(ks[0], (BATCH, SEQ, D_MODEL), 1.0),
        "mem": nrm(ks[1], (BATCH, MEM_LEN, D_MODEL), 1.0),
        "w_in": nrm(ks[2], (L, D_MODEL, IN_WIDTH), D_MODEL ** -0.5),
        "b_in": nrm(ks[3], (L, IN_WIDTH), 0.02),
        "attn_sinks": nrm(ks[4], (L, ATT_HEADS), 0.5),
        "ret_gn_g": 1.0 + nrm(ks[5], (L, RET_V_WIDTH), 0.02),
        "mem_ln_g": 1.0 + nrm(ks[6], (L, D_MODEL), 0.02),
        "mem_ln_b": nrm(ks[7], (L, D_MODEL), 0.02),
        "w_mem_kv": nrm(ks[8], (L, D_MODEL, 2 * MEM_WIDTH), D_MODEL ** -0.5),
        "w_branch_attn": nrm(ks[9], (L, ATT_Q_WIDTH, D_MODEL), BETA * ATT_Q_WIDTH ** -0.5),
        "w_branch_ret": nrm(ks[10], (L, RET_V_WIDTH, D_MODEL), BETA * RET_V_WIDTH ** -0.5),
        "w_branch_mem": nrm(ks[11], (L, MEM_WIDTH, D_MODEL), BETA * MEM_WIDTH ** -0.5),
        "w_out": nrm(ks[12], (L, D_MODEL, D_MODEL), BETA * D_MODEL ** -0.5),
        "ln1_g": 1.0 + nrm(ks[13], (L, D_MODEL), 0.02),
        "ln1_b": nrm(ks[14], (L, D_MODEL), 0.02),
        "w_peer_q": nrm(ks[15], (L, D_MODEL, PEER_HEADS * PEER_QUERY_DIM), D_MODEL ** -0.5),
        "peer_sub_keys": nrm(ks[16], (L, PEER_HEADS, 2, PEER_N_KEYS, PEER_HALF), PEER_HALF ** -0.5),
        "peer_u": nrm(ks[17], (L, PEER_N_EXPERTS, D_MODEL), D_MODEL ** -0.5),
        "peer_v": nrm(ks[18], (L, PEER_N_EXPERTS, D_MODEL), BETA * PEER_HEADS ** -0.5),
        "ln2_g": 1.0 + nrm(ks[19], (L, D_MODEL), 0.02),
        "ln2_b": nrm(ks[20], (L, D_MODEL), 0.02),
    }


def reference(x, mem, w_in, b_in, attn_sinks, ret_gn_g, mem_ln_g, mem_ln_b, w_mem_kv,
              w_branch_attn, w_branch_ret, w_branch_mem, w_out, ln1_g, ln1_b,
              w_peer_q, peer_sub_keys, peer_u, peer_v, ln2_g, ln2_b):
    for l in range(DEPTH):
        x = hybrid_layer(x, mem, w_in[l], b_in[l], attn_sinks[l], ret_gn_g[l], mem_ln_g[l], mem_ln_b[l],
                         w_mem_kv[l], w_branch_attn[l], w_branch_ret[l], w_branch_mem[l], w_out[l],
                         ln1_g[l], ln1_b[l], w_peer_q[l], peer_sub_keys[l], peer_u[l], peer_v[l],
                         ln2_g[l], ln2_b[l])
    return x
```

```python
import functools

import numpy as np
import jax
import jax.numpy as jnp
from jax import lax
from jax.experimental import pallas as pl
from jax.experimental.pallas import tpu as pltpu

F32 = jnp.float32
BF16 = jnp.bfloat16

D_MODEL = 1024
DEPTH = 1
ATT_HEADS, ATT_KV_HEADS, ATT_HEAD_DIM, ATT_BLOCK = 8, 2, 64, 128
RET_HEADS, RET_QK_DIM, RET_V_DIM, RET_CHUNK = 4, 64, 128, 128
MEM_HEADS, MEM_HEAD_DIM = 4, 128
PEER_HEADS, PEER_N_KEYS, PEER_TOPK, PEER_HALF = 8, 128, 16, 128
ALPHA = (2.0 * DEPTH) ** 0.25
LN_EPS = 1e-5
NEG_INF = float("-inf")

ATT_Q_WIDTH = ATT_HEADS * ATT_HEAD_DIM
ATT_KV_WIDTH = ATT_KV_HEADS * ATT_HEAD_DIM
RET_QK_WIDTH = RET_HEADS * RET_QK_DIM
RET_V_WIDTH = RET_HEADS * RET_V_DIM
MEM_WIDTH = MEM_HEADS * MEM_HEAD_DIM
GATE_WIDTH = 3 * D_MODEL

VMEM_LIMIT_BYTES = 56 * 1024 * 1024
LANES = 128

_C_AQ = (0, 512)
_C_AKV = (512, 768)
_C_RQ = (768, 1024)
_C_RV = (1024, 1536)
_C_RG = (1536, 2048)
_C_MQ = (2048, 2560)
_C_GATES = (2560, 5632)
_W_MAIN = 5632


def _dot(a, b):
    return jnp.dot(a, b, preferred_element_type=F32)


def _dot_nt(a, b):
    return lax.dot_general(a, b, (((1,), (1,)), ((), ())), preferred_element_type=F32)


def _layer_norm(z, g, b):
    mu = jnp.mean(z, axis=-1, keepdims=True)
    zc = z - mu
    var = jnp.mean(zc * zc, axis=-1, keepdims=True)
    return zc * lax.rsqrt(var + LN_EPS) * g + b


def _const_spec(shape):
    nd = len(shape)
    return pl.BlockSpec(shape, lambda *_: (0,) * nd, pipeline_mode=pl.Buffered(1))


def _params(sem):
    return pltpu.CompilerParams(dimension_semantics=sem, vmem_limit_bytes=VMEM_LIMIT_BYTES)


def _proj_kernel(x_ref, w_ref, b_ref, wk_ref, bk_ref, cos_ref, sin_ref, cost_ref, sint_ref,
                 aq_ref, akv_ref, rq_ref, rv_ref, rg_ref, mq_ref, gates_ref, rkt_ref):
    xb = x_ref[...].astype(BF16)

    def mm(lo, hi):
        return _dot(xb, w_ref[:, lo:hi]) + b_ref[:, lo:hi]

    aq_ref[...] = mm(*_C_AQ).astype(BF16)
    akv_ref[...] = mm(*_C_AKV).astype(BF16)
    r = mm(*_C_RQ)
    t1, t2 = r[:, :LANES], r[:, LANES:]
    c, s = cos_ref[...], sin_ref[...]
    rq_ref[:, :LANES] = (t1 * c - t2 * s).astype(BF16)
    rq_ref[:, LANES:] = (t1 * s + t2 * c).astype(BF16)
    rv_ref[...] = mm(*_C_RV).astype(BF16)
    rg_ref[...] = mm(*_C_RG).astype(BF16)
    mq_ref[...] = mm(*_C_MQ).astype(BF16)
    for j in range(GATE_WIDTH // 512):
        lo = _C_GATES[0] + j * 512
        gates_ref[:, j * 512:(j + 1) * 512] = mm(lo, lo + 512).astype(BF16)
    kt = _dot_nt(wk_ref[...], xb) + bk_ref[...]
    k1, k2 = kt[:LANES], kt[LANES:]
    ct, st = cost_ref[...], sint_ref[...]
    scale = RET_QK_DIM ** -0.5
    rkt_ref[:LANES, :] = ((k1 * ct - k2 * st) * scale).astype(BF16)
    rkt_ref[LANES:, :] = ((k1 * st + k2 * ct) * scale).astype(BF16)


def _proj(x2, w_main, b_main, w_rkt, b_rkt, cos, sin, cos_t, sin_t, seq, tm):
    t = x2.shape[0]
    nseq = seq // tm
    row = lambda w: pl.BlockSpec((tm, w), lambda i: (i, 0))
    out_shape = (
        jax.ShapeDtypeStruct((t, ATT_Q_WIDTH), BF16),
        jax.ShapeDtypeStruct((t, 2 * ATT_KV_WIDTH), BF16),
        jax.ShapeDtypeStruct((t, RET_QK_WIDTH), BF16),
        jax.ShapeDtypeStruct((t, RET_V_WIDTH), BF16),
        jax.ShapeDtypeStruct((t, RET_V_WIDTH), BF16),
        jax.ShapeDtypeStruct((t, MEM_WIDTH), BF16),
        jax.ShapeDtypeStruct((t, GATE_WIDTH), BF16),
        jax.ShapeDtypeStruct((RET_QK_WIDTH, t), BF16),
    )
    return pl.pallas_call(
        _proj_kernel,
        out_shape=out_shape,
        grid=(t // tm,),
        in_specs=[
            row(D_MODEL),
            _const_spec((D_MODEL, _W_MAIN)),
            _const_spec((1, _W_MAIN)),
            _const_spec((RET_QK_WIDTH, D_MODEL)),
            _const_spec((RET_QK_WIDTH, 1)),
            pl.BlockSpec((tm, LANES), lambda i: (i % nseq, 0)),
            pl.BlockSpec((tm, LANES), lambda i: (i % nseq, 0)),
            pl.BlockSpec((LANES, tm), lambda i: (0, i % nseq)),
            pl.BlockSpec((LANES, tm), lambda i: (0, i % nseq)),
        ],
        out_specs=(row(ATT_Q_WIDTH), row(2 * ATT_KV_WIDTH), row(RET_QK_WIDTH), row(RET_V_WIDTH),
                   row(RET_V_WIDTH), row(MEM_WIDTH), row(GATE_WIDTH),
                   pl.BlockSpec((RET_QK_WIDTH, tm), lambda i: (0, i))),
        compiler_params=_params(("parallel",)),
        name="proj",
    )(x2, w_main, b_main, w_rkt, b_rkt, cos, sin, cos_t, sin_t)


def _attn_kernel(sink_ref, q_ref, kv_ref, kvp_ref, o_ref, kvs_ref):
    i = pl.program_id(1)
    tq = q_ref.shape[0]
    blk = ATT_BLOCK
    kvs_ref[0:blk, :] = kvp_ref[...]
    kvs_ref[blk:blk + tq, :] = kv_ref[...]
    qi = lax.broadcasted_iota(jnp.int32, (blk, 2 * blk), 0)
    kj = lax.broadcasted_iota(jnp.int32, (blk, 2 * blk), 1)
    base_mask = ((kj < blk) & (kj > qi)) | ((kj >= blk) & ((kj - blk) <= qi))
    first_key = jnp.where(i > 0, 0, blk)
    lane = lax.broadcasted_iota(jnp.int32, (blk, LANES), 1)
    low = lane < ATT_HEAD_DIM
    qmask = (low.astype(F32).astype(BF16), jnp.logical_not(low).astype(F32).astype(BF16))
    rep = ATT_HEADS // ATT_KV_HEADS
    scale = ATT_HEAD_DIM ** -0.5
    for b in range(tq // blk):
        mask = (base_mask & (kj >= first_key)) if b == 0 else base_mask
        kv = kvs_ref[b * blk:(b + 2) * blk, :]
        k, v = kv[:, :LANES], kv[:, LANES:]
        for r in range(rep):
            qc = q_ref[b * blk:(b + 1) * blk, r * LANES:(r + 1) * LANES]
            outs = []
            for g in range(ATT_KV_HEADS):
                qm = qc * qmask[g]
                l = _dot_nt(qm, k) * scale
                l = jnp.where(mask, l, NEG_INF)
                sink = sink_ref[g * rep + r]
                m = jnp.maximum(jnp.max(l, axis=-1, keepdims=True), sink)
                p = jnp.exp(l - m)
                den = jnp.sum(p, axis=-1, keepdims=True) + jnp.exp(sink - m)
                outs.append(_dot(p.astype(BF16), v) / den)
            o_ref[b * blk:(b + 1) * blk, r * LANES:(r + 1) * LANES] = (
                jnp.where(low, outs[0], outs[1]).astype(BF16))


def _attention(aq, akv, sinks, batch, seq, tq):
    t = aq.shape[0]
    nq = seq // tq
    per = tq // ATT_BLOCK
    return pl.pallas_call(
        _attn_kernel,
        out_shape=jax.ShapeDtypeStruct((t, ATT_Q_WIDTH), BF16),
        grid=(batch, nq),
        in_specs=[
            pl.BlockSpec(memory_space=pltpu.SMEM),
            pl.BlockSpec((tq, ATT_Q_WIDTH), lambda b, i: (b * nq + i, 0)),
            pl.BlockSpec((tq, 2 * ATT_KV_WIDTH), lambda b, i: (b * nq + i, 0)),
            pl.BlockSpec((ATT_BLOCK, 2 * ATT_KV_WIDTH),
                         lambda b, i: (jnp.maximum((b * nq + i) * per - 1, 0), 0)),
        ],
        out_specs=pl.BlockSpec((tq, ATT_Q_WIDTH), lambda b, i: (b * nq + i, 0)),
        scratch_shapes=[pltpu.VMEM((tq + ATT_BLOCK, 2 * ATT_KV_WIDTH), BF16)],
        compiler_params=_params(("parallel", "parallel")),
        name="attn",
    )(sinks, aq, akv, akv)


def _ret_kernel(chunk_decay, q_ref, kt_ref, v_ref, g_ref, gn_ref, din_ref, wk_ref, wq_ref,
                o_ref, state_ref):
    i = pl.program_id(1)
    tc = q_ref.shape[0]
    ck = RET_CHUNK

    @pl.when(i == 0)
    def _():
        state_ref[...] = jnp.zeros_like(state_ref)

    half = RET_QK_DIM // 2
    rowi = lax.broadcasted_iota(jnp.int32, (RET_QK_WIDTH, ck), 0)
    for c in range(tc // ck):
        rows = slice(c * ck, (c + 1) * ck)
        q = q_ref[rows, :]
        kt = kt_ref[:, rows].astype(F32)
        for h in range(RET_HEADS):
            cols = slice(h * RET_V_DIM, (h + 1) * RET_V_DIM)
            kth = jnp.where((rowi % LANES) // half == h, kt, 0.0)
            vh = v_ref[rows, cols]
            inner = _dot(q, kth.astype(BF16)) * din_ref[h]
            o_in = _dot(inner.astype(BF16), vh)
            st = state_ref[h]
            o_x = _dot(q, st.astype(BF16)) * wq_ref[h]
            state_ref[h] = chunk_decay[h] * st + _dot((kth * wk_ref[h]).astype(BF16), vh)
            o = o_in + o_x
            mu = jnp.mean(o, axis=-1, keepdims=True)
            oc = o - mu
            var = jnp.mean(oc * oc, axis=-1, keepdims=True)
            y = oc * lax.rsqrt(var + LN_EPS) * gn_ref[:, cols]
            gate = g_ref[rows, cols].astype(F32)
            o_ref[rows, cols] = (y * (gate / (1.0 + jnp.exp(-gate)))).astype(BF16)


def _retention(rq, rkt, rv, rg, gn, batch, seq, tc):
    t = rq.shape[0]
    nc = seq // tc
    log_gamma = np.log(1.0 - 2.0 ** (-5.0 - np.arange(RET_HEADS, dtype=np.float64)))
    idx = np.arange(RET_CHUNK, dtype=np.float64)
    diff = idx[:, None] - idx[None, :]
    decay_in = np.where(diff >= 0, np.exp(log_gamma[:, None, None] * np.maximum(diff, 0.0)), 0.0)
    w_k = np.exp(log_gamma[:, None] * (RET_CHUNK - 1.0 - idx)[None, :])[:, None, :]
    w_q = np.broadcast_to(np.exp(log_gamma[:, None] * (idx + 1.0)[None, :])[:, :, None],
                          (RET_HEADS, RET_CHUNK, RET_V_DIM))
    chunk_decay = tuple(float(v) for v in np.exp(log_gamma * RET_CHUNK))
    row = lambda w: pl.BlockSpec((tc, w), lambda b, i: (b * nc + i, 0))
    return pl.pallas_call(
        functools.partial(_ret_kernel, chunk_decay),
        out_shape=jax.ShapeDtypeStruct((t, RET_V_WIDTH), BF16),
        grid=(batch, nc),
        in_specs=[
            row(RET_QK_WIDTH),
            pl.BlockSpec((RET_QK_WIDTH, tc), lambda b, i: (0, b * nc + i)),
            row(RET_V_WIDTH),
            row(RET_V_WIDTH),
            _const_spec((1, RET_V_WIDTH)),
            _const_spec((RET_HEADS, RET_CHUNK, RET_CHUNK)),
            _const_spec((RET_HEADS, 1, RET_CHUNK)),
            _const_spec((RET_HEADS, RET_CHUNK, RET_V_DIM)),
        ],
        out_specs=row(RET_V_WIDTH),
        scratch_shapes=[pltpu.VMEM((RET_HEADS, RET_QK_WIDTH, RET_V_DIM), F32)],
        compiler_params=_params(("parallel", "arbitrary")),
        name="ret",
    )(rq, rkt, rv, rg, gn, jnp.asarray(decay_in, F32), jnp.asarray(w_k, F32), jnp.asarray(w_q, F32))


def _memkv_kernel(mem_ref, g_ref, b_ref, w_ref, o_ref):
    m = _layer_norm(mem_ref[...], g_ref[...], b_ref[...])
    o_ref[...] = _dot(m.astype(BF16), w_ref[...]).astype(BF16)


def _memkv(mem2, g, b, w, batch, mlen):
    return pl.pallas_call(
        _memkv_kernel,
        out_shape=jax.ShapeDtypeStruct((batch * mlen, 2 * MEM_WIDTH), BF16),
        grid=(batch,),
        in_specs=[pl.BlockSpec((mlen, D_MODEL), lambda b: (b, 0)), _const_spec((1, D_MODEL)),
                  _const_spec((1, D_MODEL)), _const_spec((D_MODEL, 2 * MEM_WIDTH))],
        out_specs=pl.BlockSpec((mlen, 2 * MEM_WIDTH), lambda b: (b, 0)),
        compiler_params=_params(("parallel",)),
        name="memkv",
    )(mem2, g, b, w)


def _xattn_kernel(q_ref, kv_ref, o_ref):
    scale = MEM_HEAD_DIM ** -0.5
    for h in range(MEM_HEADS):
        cols = slice(h * MEM_HEAD_DIM, (h + 1) * MEM_HEAD_DIM)
        vcols = slice(MEM_WIDTH + h * MEM_HEAD_DIM, MEM_WIDTH + (h + 1) * MEM_HEAD_DIM)
        l = _dot_nt(q_ref[:, cols], kv_ref[:, cols]) * scale
        p = jnp.exp(l - jnp.max(l, axis=-1, keepdims=True))
        den = jnp.sum(p, axis=-1, keepdims=True)
        o_ref[:, cols] = (_dot(p.astype(BF16), kv_ref[:, vcols]) / den).astype(BF16)


def _xattn(mq, mkv, batch, seq, mlen, tq):
    t = mq.shape[0]
    nq = seq // tq
    return pl.pallas_call(
        _xattn_kernel,
        out_shape=jax.ShapeDtypeStruct((t, MEM_WIDTH), BF16),
        grid=(batch, nq),
        in_specs=[pl.BlockSpec((tq, MEM_WIDTH), lambda b, i: (b * nq + i, 0)),
                  pl.BlockSpec((mlen, 2 * MEM_WIDTH), lambda b, i: (b, 0))],
        out_specs=pl.BlockSpec((tq, MEM_WIDTH), lambda b, i: (b * nq + i, 0)),
        compiler_params=_params(("parallel", "parallel")),
        name="xattn",
    )(mq, mkv)


def _merge_kernel(a_ref, r_ref, m_ref, gates_ref, x_ref, wa_ref, wr_ref, wm_ref, wo_ref,
                  g_ref, b_ref, h_ref):
    def sig(j):
        z = gates_ref[:, j * D_MODEL:(j + 1) * D_MODEL].astype(F32)
        return 1.0 / (1.0 + jnp.exp(-z))

    merged = sig(0) * _dot(a_ref[...], wa_ref[...])
    merged += sig(1) * _dot(r_ref[...], wr_ref[...])
    merged += sig(2) * _dot(m_ref[...], wm_ref[...])
    z = ALPHA * x_ref[...] + _dot(merged.astype(BF16), wo_ref[...])
    h_ref[...] = _layer_norm(z, g_ref[...], b_ref[...])


def _merge(br_a, br_r, br_m, gates, x2, wa, wr, wm, wo, g, b, tm):
    t = x2.shape[0]
    row = lambda w: pl.BlockSpec((tm, w), lambda i: (i, 0))
    return pl.pallas_call(
        _merge_kernel,
        out_shape=jax.ShapeDtypeStruct((t, D_MODEL), F32),
        grid=(t // tm,),
        in_specs=[row(ATT_Q_WIDTH), row(RET_V_WIDTH), row(MEM_WIDTH), row(GATE_WIDTH), row(D_MODEL),
                  _const_spec((ATT_Q_WIDTH, D_MODEL)), _const_spec((RET_V_WIDTH, D_MODEL)),
                  _const_spec((MEM_WIDTH, D_MODEL)), _const_spec((D_MODEL, D_MODEL)),
                  _const_spec((1, D_MODEL)), _const_spec((1, D_MODEL))],
        out_specs=row(D_MODEL),
        compiler_params=_params(("parallel",)),
        name="merge",
    )(br_a, br_r, br_m, gates, x2, wa, wr, wm, wo, g, b)


_N_TOP = PEER_TOPK + 1
_TOP_ROWS = 24
_ST_TAU, _ST_M1, _ST_M2, _ST_Z = 0, 1, 2, 3
_ST_ROWS = 8


def _sel_kernel(h_ref, wq_ref, keys_ref, ht_ref, s1_ref, s2_ref, st_ref, qt_s, sc_s, tops_s):
    tt = h_ref.shape[0]
    ht = h_ref[...].T.astype(BF16)
    ht_ref[...] = ht
    qt_s[...] = _dot(wq_ref[...], ht).astype(BF16)
    nk = PEER_N_KEYS
    for hp in range(2 * PEER_HEADS):
        s = _dot(keys_ref[hp], qt_s[hp * PEER_HALF:(hp + 1) * PEER_HALF, :])
        hd, p = divmod(hp, 2)
        (s1_ref if p == 0 else s2_ref)[hd * nk:(hd + 1) * nk, :] = s
        sc_s[hp] = s

    row8 = lax.broadcasted_iota(jnp.int32, (8, LANES), 0)
    ngroups = tt // LANES
    ntile = _TOP_ROWS // 8

    def top_lists(hp, carry):
        for lg in range(ngroups):
            lanes = slice(lg * LANES, (lg + 1) * LANES)
            s = sc_s[hp, :, lanes]
            tops = [jnp.full((8, LANES), NEG_INF, F32) for _ in range(ntile)]
            for k in range(_N_TOP):
                m = jnp.max(s, axis=0, keepdims=True)
                s = jnp.where(s == m, NEG_INF, s)
                tops[k // 8] = jnp.where(row8 == (k % 8), m, tops[k // 8])
            for j, tile in enumerate(tops):
                tops_s[hp * ntile + j, :, lanes] = tile
        return carry

    lax.fori_loop(0, 2 * PEER_HEADS, top_lists, 0)

    def head_stats(hd, carry):
        for lg in range(ngroups):
            lanes = slice(lg * LANES, (lg + 1) * LANES)
            t2 = [tops_s[(2 * hd + 1) * ntile + j, :, lanes] for j in range(ntile)]
            first = tops_s[2 * hd * ntile, 0:1, lanes]
            cands = [first + t for t in t2]
            for a in range(1, _N_TOP):
                lim = _N_TOP // (a + 1)
                ta = tops_s[2 * hd * ntile + a // 8, a % 8:a % 8 + 1, lanes]
                cands.append(jnp.where(row8 < lim, ta + t2[0], NEG_INF))
            m0 = None
            z = jnp.zeros((1, LANES), F32)
            for k in range(_N_TOP):
                m = functools.reduce(jnp.maximum, cands)
                m = jnp.max(m, axis=0, keepdims=True)
                cands = [jnp.where(c == m, NEG_INF, c) for c in cands]
                if k == 0:
                    m0 = m
                if k < PEER_TOPK:
                    z = z + jnp.exp(m - m0)
                if k == PEER_TOPK - 1:
                    c16 = m
                if k == PEER_TOPK:
                    c17 = m
            tile = jnp.zeros((_ST_ROWS, LANES), F32)
            for r, val in ((_ST_TAU, 0.5 * (c16 + c17)), (_ST_M1, first), (_ST_M2, t2[0][0:1, :]),
                           (_ST_Z, z)):
                tile = jnp.where(row8 == r, val, tile)
            st_ref[hd, :, lanes] = tile
        return carry

    lax.fori_loop(0, PEER_HEADS, head_stats, 0)


def _peer_select(h2, wq_t, keys, tt):
    t = h2.shape[0]
    nrow = PEER_HEADS * PEER_N_KEYS
    col = lambda r: pl.BlockSpec((r, tt), lambda i: (0, i))
    return pl.pallas_call(
        _sel_kernel,
        out_shape=(jax.ShapeDtypeStruct((D_MODEL, t), BF16),
                   jax.ShapeDtypeStruct((nrow, t), F32),
                   jax.ShapeDtypeStruct((nrow, t), F32),
                   jax.ShapeDtypeStruct((PEER_HEADS, _ST_ROWS, t), F32)),
        grid=(t // tt,),
        in_specs=[pl.BlockSpec((tt, D_MODEL), lambda i: (i, 0)),
                  _const_spec((2 * PEER_HEADS * PEER_HALF, D_MODEL)),
                  _const_spec((2 * PEER_HEADS, PEER_N_KEYS, PEER_HALF))],
        out_specs=(col(D_MODEL), col(nrow), col(nrow),
                   pl.BlockSpec((PEER_HEADS, _ST_ROWS, tt), lambda i: (0, 0, i))),
        scratch_shapes=[pltpu.VMEM((2 * PEER_HEADS * PEER_HALF, tt), BF16),
                        pltpu.VMEM((2 * PEER_HEADS, PEER_N_KEYS, tt), F32),
                        pltpu.VMEM((2 * PEER_HEADS * _TOP_ROWS // 8, 8, tt), F32)],
        compiler_params=_params(("parallel",)),
        name="peer_sel",
    )(h2, wq_t, keys)


def _dense_kernel(h_ref, ht_ref, s1_ref, s2_ref, st_ref, u_ref, vt_ref, g_ref, b_ref, y_ref,
                  e2_s, thr_s, e1_s, acc_s, gs_s):
    c = pl.program_id(1)
    nk = PEER_N_KEYS
    ec, tt = gs_s.shape
    assert ec == 8 * nk
    ngrp = nk // 8

    @pl.when(c == 0)
    def _():
        for hd in range(PEER_HEADS):
            rows = slice(hd * nk, (hd + 1) * nk)
            tau = st_ref[hd, _ST_TAU:_ST_TAU + 1, :]
            m1 = st_ref[hd, _ST_M1:_ST_M1 + 1, :]
            m2 = st_ref[hd, _ST_M2:_ST_M2 + 1, :]
            z = st_ref[hd, _ST_Z:_ST_Z + 1, :]
            e2_s[rows, :] = jnp.exp(s2_ref[rows, :] - m2)
            thr_s[hd * ngrp:(hd + 1) * ngrp] = (tau - s1_ref[rows, :]).reshape(ngrp, 8, tt)
            e1_s[hd * ngrp:(hd + 1) * ngrp] = (jnp.exp(s1_ref[rows, :] - m1) / z).reshape(ngrp, 8, tt)
        acc_s[...] = jnp.zeros_like(acc_s)

    a = _dot(u_ref[...], ht_ref[...])
    ga = 0.5 * a * (1.0 + lax.erf(a * (2.0 ** -0.5)))
    for ib in range(8):
        for lg in range(tt // LANES):
            lanes = slice(lg * LANES, (lg + 1) * LANES)
            w = jnp.zeros((nk, LANES), F32)
            for hd in range(PEER_HEADS):
                rows = slice(hd * nk, (hd + 1) * nk)
                thr = thr_s[hd * ngrp + c, ib:ib + 1, lanes]
                e1 = e1_s[hd * ngrp + c, ib:ib + 1, lanes]
                w = w + jnp.where(s2_ref[rows, lanes] >= thr, e2_s[rows, lanes], 0.0) * e1
            gs_s[ib * nk:(ib + 1) * nk, lanes] = (w * ga[ib * nk:(ib + 1) * nk, lanes]).astype(BF16)
    acc_s[...] += _dot(vt_ref[...], gs_s[...])

    @pl.when(c == pl.num_programs(1) - 1)
    def _():
        z = ALPHA * h_ref[...] + acc_s[...].T
        y_ref[...] = _layer_norm(z, g_ref[...], b_ref[...])


def _peer_dense(h2, ht, s1, s2, st, u, vt, g, b, tt, ec):
    t = h2.shape[0]
    ne = u.shape[0]
    nrow = PEER_HEADS * PEER_N_KEYS
    col = lambda r: pl.BlockSpec((r, tt), lambda i, c: (0, i))
    return pl.pallas_call(
        _dense_kernel,
        out_shape=jax.ShapeDtypeStruct((t, D_MODEL), F32),
        grid=(t // tt, ne // ec),
        in_specs=[pl.BlockSpec((tt, D_MODEL), lambda i, c: (i, 0)),
                  col(D_MODEL), col(nrow), col(nrow),
                  pl.BlockSpec((PEER_HEADS, _ST_ROWS, tt), lambda i, c: (0, 0, i)),
                  pl.BlockSpec((ec, D_MODEL), lambda i, c: (c, 0)),
                  pl.BlockSpec((D_MODEL, ec), lambda i, c: (0, c)),
                  pl.BlockSpec((1, D_MODEL), lambda i, c: (0, 0)),
                  pl.BlockSpec((1, D_MODEL), lambda i, c: (0, 0))],
        out_specs=pl.BlockSpec((tt, D_MODEL), lambda i, c: (i, 0)),
        scratch_shapes=[pltpu.VMEM((nrow, tt), F32), pltpu.VMEM((nrow // 8, 8, tt), F32),
                        pltpu.VMEM((nrow // 8, 8, tt), F32), pltpu.VMEM((D_MODEL, tt), F32),
                        pltpu.VMEM((ec, tt), BF16)],
        compiler_params=_params(("parallel", "arbitrary")),
        name="peer_dense",
    )(h2, ht, s1, s2, st, u, vt, g, b)


def _prep_w_in(w_in, b_in):
    sizes = (ATT_Q_WIDTH, ATT_KV_WIDTH, ATT_KV_WIDTH, RET_QK_WIDTH, RET_QK_WIDTH, RET_V_WIDTH,
             RET_V_WIDTH, MEM_WIDTH, GATE_WIDTH)
    offs = np.cumsum(sizes)[:-1].tolist()
    wb = jnp.concatenate([w_in, b_in[None, :]], axis=0)
    aq, ak, av, rq, rk, rv, rg, mq, gates = jnp.split(wb, offs, axis=1)
    rows = wb.shape[0]
    rep = ATT_HEADS // ATT_KV_HEADS
    half = RET_QK_DIM // 2
    aq = aq.reshape(rows, ATT_KV_HEADS, rep, ATT_HEAD_DIM).transpose(0, 2, 1, 3).reshape(rows, -1)

    def halves_first(w):
        return w.reshape(rows, RET_HEADS, 2, half).transpose(0, 2, 1, 3).reshape(rows, -1)

    main = jnp.concatenate([aq, ak, av, halves_first(rq), rv, rg, mq, gates], axis=1)
    assert main.shape[1] == _W_MAIN
    rk = halves_first(rk)
    return (main[:-1].astype(BF16), main[-1:], rk[:-1].T.astype(BF16), rk[-1][:, None])


def _prep_w_branch_attn(w):
    rep = ATT_HEADS // ATT_KV_HEADS
    return (w.reshape(ATT_KV_HEADS, rep, ATT_HEAD_DIM, -1).transpose(1, 0, 2, 3)
            .reshape(ATT_Q_WIDTH, -1).astype(BF16))


def _rotation_tables(seq):
    half = RET_QK_DIM // 2
    pos = jnp.arange(seq, dtype=F32)
    theta = 1.0 / (10000.0 ** jnp.linspace(0.0, 1.0, half, dtype=F32))
    ang = pos[:, None] * theta[None, :]
    cos = jnp.tile(jnp.cos(ang), (1, LANES // half))
    sin = jnp.tile(jnp.sin(ang), (1, LANES // half))
    return cos, sin, cos.T, sin.T


def _layer(x, mem, w_in, b_in, attn_sinks, ret_gn_g, mem_ln_g, mem_ln_b, w_mem_kv,
           w_branch_attn, w_branch_ret, w_branch_mem, w_out, ln1_g, ln1_b,
           w_peer_q, peer_sub_keys, peer_u, peer_v, ln2_g, ln2_b):
    batch, seq, d = x.shape
    mlen = mem.shape[1]
    t = batch * seq
    tile = min(512, seq)
    x2 = x.reshape(t, d)
    row = lambda v: v.reshape(1, -1).astype(F32)

    w_main, b_main, w_rkt, b_rkt = _prep_w_in(w_in, b_in)
    cos, sin, cos_t, sin_t = _rotation_tables(seq)
    aq, akv, rq, rv, rg, mq, gates, rkt = _proj(
        x2, w_main, b_main, w_rkt, b_rkt, cos, sin, cos_t, sin_t, seq, tile)

    br_a = _attention(aq, akv, attn_sinks.astype(F32), batch, seq, tile)
    br_r = _retention(rq, rkt, rv, rg, row(ret_gn_g), batch, seq, tile)
    mkv = _memkv(mem.reshape(batch * mlen, d), row(mem_ln_g), row(mem_ln_b),
                 w_mem_kv.astype(BF16), batch, mlen)
    br_m = _xattn(mq, mkv, batch, seq, mlen, tile)

    h2 = _merge(br_a, br_r, br_m, gates, x2,
                _prep_w_branch_attn(w_branch_attn), w_branch_ret.astype(BF16),
                w_branch_mem.astype(BF16), w_out.astype(BF16), row(ln1_g), row(ln1_b), tile)

    wq_t = w_peer_q.T.astype(BF16)
    keys = peer_sub_keys.reshape(2 * PEER_HEADS, PEER_N_KEYS, PEER_HALF).astype(BF16)
    ht, s1, s2, st = _peer_select(h2, wq_t, keys, tile)
    y2 = _peer_dense(h2, ht, s1, s2, st, peer_u.astype(BF16), peer_v.T.astype(BF16),
                     row(ln2_g), row(ln2_b), tile, 8 * PEER_N_KEYS)
    return y2.reshape(batch, seq, d)


def kernel(x, mem, w_in, b_in, attn_sinks, ret_gn_g, mem_ln_g, mem_ln_b, w_mem_kv, w_branch_attn,
           w_branch_ret, w_branch_mem, w_out, ln1_g, ln1_b, w_peer_q, peer_sub_keys, peer_u, peer_v,
           ln2_g, ln2_b):
    for l in range(DEPTH):
        x = _layer(x, mem, w_in[l], b_in[l], attn_sinks[l], ret_gn_g[l], mem_ln_g[l], mem_ln_b[l],
                   w_mem_kv[l], w_branch_attn[l], w_branch_ret[l], w_branch_mem[l], w_out[l],
                   ln1_g[l], ln1_b[l], w_peer_q[l], peer_sub_keys[l], peer_u[l], peer_v[l],
                   ln2_g[l], ln2_b[l])
    return x
```

```python
import functools

import numpy as np
import jax
import jax.numpy as jnp
from jax import lax
from jax.experimental import pallas as pl
from jax.experimental.pallas import tpu as pltpu

F32 = jnp.float32
BF16 = jnp.bfloat16

D_MODEL = 1024
DEPTH = 1
ATT_HEADS, ATT_KV_HEADS, ATT_HEAD_DIM, ATT_BLOCK = 8, 2, 64, 128
RET_HEADS, RET_QK_DIM, RET_V_DIM, RET_CHUNK = 4, 64, 128, 128
MEM_HEADS, MEM_HEAD_DIM = 4, 128
PEER_HEADS, PEER_N_KEYS, PEER_TOPK, PEER_HALF = 8, 128, 16, 128
ALPHA = (2.0 * DEPTH) ** 0.25
LN_EPS = 1e-5
NEG_INF = float("-inf")

ATT_Q_WIDTH = ATT_HEADS * ATT_HEAD_DIM
ATT_KV_WIDTH = ATT_KV_HEADS * ATT_HEAD_DIM
RET_QK_WIDTH = RET_HEADS * RET_QK_DIM
RET_V_WIDTH = RET_HEADS * RET_V_DIM
MEM_WIDTH = MEM_HEADS * MEM_HEAD_DIM
GATE_WIDTH = 3 * D_MODEL

VMEM_LIMIT_BYTES = 56 * 1024 * 1024
LANES = 128

_C_AQ = (0, 512)
_C_AKV = (512, 768)
_C_RQ = (768, 1024)
_C_RV = (1024, 1536)
_C_RG = (1536, 2048)
_C_MQ = (2048, 2560)
_C_GATES = (2560, 5632)
_W_MAIN = 5632


def _dot(a, b):
    return jnp.dot(a, b, preferred_element_type=F32)


def _dot_nt(a, b):
    return lax.dot_general(a, b, (((1,), (1,)), ((), ())), preferred_element_type=F32)


def _layer_norm(z, g, b):
    mu = jnp.mean(z, axis=-1, keepdims=True)
    zc = z - mu
    var = jnp.mean(zc * zc, axis=-1, keepdims=True)
    return zc * lax.rsqrt(var + LN_EPS) * g + b


def _const_spec(shape):
    nd = len(shape)
    return pl.BlockSpec(shape, lambda *_: (0,) * nd, pipeline_mode=pl.Buffered(1))


def _params(sem, flags=None):
    return pltpu.CompilerParams(dimension_semantics=sem, vmem_limit_bytes=VMEM_LIMIT_BYTES,
                                flags=flags)


def _proj_kernel(x_ref, w_ref, b_ref, wk_ref, bk_ref, cos_ref, sin_ref, cost_ref, sint_ref,
                 aq_ref, akv_ref, rq_ref, rv_ref, rg_ref, mq_ref, gates_ref, rkt_ref):
    xb = x_ref[...].astype(BF16)

    def mm(lo, hi):
        return _dot(xb, w_ref[:, lo:hi]) + b_ref[:, lo:hi]

    aq_ref[...] = mm(*_C_AQ).astype(BF16)
    akv_ref[...] = mm(*_C_AKV).astype(BF16)
    r = mm(*_C_RQ)
    t1, t2 = r[:, :LANES], r[:, LANES:]
    c, s = cos_ref[...], sin_ref[...]
    rq_ref[:, :LANES] = (t1 * c - t2 * s).astype(BF16)
    rq_ref[:, LANES:] = (t1 * s + t2 * c).astype(BF16)
    rv_ref[...] = mm(*_C_RV).astype(BF16)
    rg_ref[...] = mm(*_C_RG).astype(BF16)
    mq_ref[...] = mm(*_C_MQ).astype(BF16)
    for j in range(GATE_WIDTH // 512):
        lo = _C_GATES[0] + j * 512
        gates_ref[:, j * 512:(j + 1) * 512] = mm(lo, lo + 512).astype(BF16)
    kt = _dot_nt(wk_ref[...], xb) + bk_ref[...]
    k1, k2 = kt[:LANES], kt[LANES:]
    ct, st = cost_ref[...], sint_ref[...]
    scale = RET_QK_DIM ** -0.5
    rkt_ref[:LANES, :] = ((k1 * ct - k2 * st) * scale).astype(BF16)
    rkt_ref[LANES:, :] = ((k1 * st + k2 * ct) * scale).astype(BF16)


def _proj(x2, w_main, b_main, w_rkt, b_rkt, cos, sin, cos_t, sin_t, seq, tm):
    t = x2.shape[0]
    nseq = seq // tm
    row = lambda w: pl.BlockSpec((tm, w), lambda i: (i, 0))
    out_shape = (
        jax.ShapeDtypeStruct((t, ATT_Q_WIDTH), BF16),
        jax.ShapeDtypeStruct((t, 2 * ATT_KV_WIDTH), BF16),
        jax.ShapeDtypeStruct((t, RET_QK_WIDTH), BF16),
        jax.ShapeDtypeStruct((t, RET_V_WIDTH), BF16),
        jax.ShapeDtypeStruct((t, RET_V_WIDTH), BF16),
        jax.ShapeDtypeStruct((t, MEM_WIDTH), BF16),
        jax.ShapeDtypeStruct((t, GATE_WIDTH), BF16),
        jax.ShapeDtypeStruct((RET_QK_WIDTH, t), BF16),
    )
    return pl.pallas_call(
        _proj_kernel,
        out_shape=out_shape,
        grid=(t // tm,),
        in_specs=[
            row(D_MODEL),
            _const_spec((D_MODEL, _W_MAIN)),
            _const_spec((1, _W_MAIN)),
            _const_spec((RET_QK_WIDTH, D_MODEL)),
            _const_spec((RET_QK_WIDTH, 1)),
            pl.BlockSpec((tm, LANES), lambda i: (i % nseq, 0)),
            pl.BlockSpec((tm, LANES), lambda i: (i % nseq, 0)),
            pl.BlockSpec((LANES, tm), lambda i: (0, i % nseq)),
            pl.BlockSpec((LANES, tm), lambda i: (0, i % nseq)),
        ],
        out_specs=(row(ATT_Q_WIDTH), row(2 * ATT_KV_WIDTH), row(RET_QK_WIDTH), row(RET_V_WIDTH),
                   row(RET_V_WIDTH), row(MEM_WIDTH), row(GATE_WIDTH),
                   pl.BlockSpec((RET_QK_WIDTH, tm), lambda i: (0, i))),
        compiler_params=_params(("parallel",)),
        name="proj",
    )(x2, w_main, b_main, w_rkt, b_rkt, cos, sin, cos_t, sin_t)


def _attn_kernel(sink_ref, q_ref, kv_ref, kvp_ref, o_ref, kvs_ref):
    i = pl.program_id(1)
    tq = q_ref.shape[0]
    blk = ATT_BLOCK
    kvs_ref[0:blk, :] = kvp_ref[...]
    kvs_ref[blk:blk + tq, :] = kv_ref[...]
    qi = lax.broadcasted_iota(jnp.int32, (blk, 2 * blk), 0)
    kj = lax.broadcasted_iota(jnp.int32, (blk, 2 * blk), 1)
    base_mask = ((kj < blk) & (kj > qi)) | ((kj >= blk) & ((kj - blk) <= qi))
    first_key = jnp.where(i > 0, 0, blk)
    lane = lax.broadcasted_iota(jnp.int32, (blk, LANES), 1)
    low = lane < ATT_HEAD_DIM
    qmask = (low.astype(F32).astype(BF16), jnp.logical_not(low).astype(F32).astype(BF16))
    rep = ATT_HEADS // ATT_KV_HEADS
    scale = ATT_HEAD_DIM ** -0.5
    for b in range(tq // blk):
        mask = (base_mask & (kj >= first_key)) if b == 0 else base_mask
        kv = kvs_ref[b * blk:(b + 2) * blk, :]
        k, v = kv[:, :LANES], kv[:, LANES:]
        for r in range(rep):
            qc = q_ref[b * blk:(b + 1) * blk, r * LANES:(r + 1) * LANES]
            outs = []
            for g in range(ATT_KV_HEADS):
                qm = qc * qmask[g]
                l = _dot_nt(qm, k) * scale
                l = jnp.where(mask, l, NEG_INF)
                sink = sink_ref[g * rep + r]
                m = jnp.maximum(jnp.max(l, axis=-1, keepdims=True), sink)
                p = jnp.exp(l - m)
                den = jnp.sum(p, axis=-1, keepdims=True) + jnp.exp(sink - m)
                outs.append(_dot(p.astype(BF16), v) / den)
            o_ref[b * blk:(b + 1) * blk, r * LANES:(r + 1) * LANES] = (
                jnp.where(low, outs[0], outs[1]).astype(BF16))


def _attention(aq, akv, sinks, batch, seq, tq):
    t = aq.shape[0]
    nq = seq // tq
    per = tq // ATT_BLOCK
    return pl.pallas_call(
        _attn_kernel,
        out_shape=jax.ShapeDtypeStruct((t, ATT_Q_WIDTH), BF16),
        grid=(batch, nq),
        in_specs=[
            pl.BlockSpec(memory_space=pltpu.SMEM),
            pl.BlockSpec((tq, ATT_Q_WIDTH), lambda b, i: (b * nq + i, 0)),
            pl.BlockSpec((tq, 2 * ATT_KV_WIDTH), lambda b, i: (b * nq + i, 0)),
            pl.BlockSpec((ATT_BLOCK, 2 * ATT_KV_WIDTH),
                         lambda b, i: (jnp.maximum((b * nq + i) * per - 1, 0), 0)),
        ],
        out_specs=pl.BlockSpec((tq, ATT_Q_WIDTH), lambda b, i: (b * nq + i, 0)),
        scratch_shapes=[pltpu.VMEM((tq + ATT_BLOCK, 2 * ATT_KV_WIDTH), BF16)],
        compiler_params=_params(("parallel", "parallel")),
        name="attn",
    )(sinks, aq, akv, akv)


def _ret_kernel(chunk_decay, q_ref, kt_ref, v_ref, g_ref, gn_ref, din_ref, wk_ref, wq_ref,
                o_ref, state_ref):
    i = pl.program_id(1)
    tc = q_ref.shape[0]
    ck = RET_CHUNK

    @pl.when(i == 0)
    def _():
        state_ref[...] = jnp.zeros_like(state_ref)

    half = RET_QK_DIM // 2
    rowi = lax.broadcasted_iota(jnp.int32, (RET_QK_WIDTH, ck), 0)
    for c in range(tc // ck):
        rows = slice(c * ck, (c + 1) * ck)
        q = q_ref[rows, :]
        kt = kt_ref[:, rows].astype(F32)
        for h in range(RET_HEADS):
            cols = slice(h * RET_V_DIM, (h + 1) * RET_V_DIM)
            kth = jnp.where((rowi % LANES) // half == h, kt, 0.0)
            vh = v_ref[rows, cols]
            inner = _dot(q, kth.astype(BF16)) * din_ref[h]
            o_in = _dot(inner.astype(BF16), vh)
            st = state_ref[h]
            o_x = _dot(q, st.astype(BF16)) * wq_ref[h]
            state_ref[h] = chunk_decay[h] * st + _dot((kth * wk_ref[h]).astype(BF16), vh)
            o = o_in + o_x
            mu = jnp.mean(o, axis=-1, keepdims=True)
            oc = o - mu
            var = jnp.mean(oc * oc, axis=-1, keepdims=True)
            y = oc * lax.rsqrt(var + LN_EPS) * gn_ref[:, cols]
            gate = g_ref[rows, cols].astype(F32)
            o_ref[rows, cols] = (y * (gate / (1.0 + jnp.exp(-gate)))).astype(BF16)


def _retention(rq, rkt, rv, rg, gn, batch, seq, tc):
    t = rq.shape[0]
    nc = seq // tc
    log_gamma = np.log(1.0 - 2.0 ** (-5.0 - np.arange(RET_HEADS, dtype=np.float64)))
    idx = np.arange(RET_CHUNK, dtype=np.float64)
    diff = idx[:, None] - idx[None, :]
    decay_in = np.where(diff >= 0, np.exp(log_gamma[:, None, None] * np.maximum(diff, 0.0)), 0.0)
    w_k = np.exp(log_gamma[:, None] * (RET_CHUNK - 1.0 - idx)[None, :])[:, None, :]
    w_q = np.broadcast_to(np.exp(log_gamma[:, None] * (idx + 1.0)[None, :])[:, :, None],
                          (RET_HEADS, RET_CHUNK, RET_V_DIM))
    chunk_decay = tuple(float(v) for v in np.exp(log_gamma * RET_CHUNK))
    row = lambda w: pl.BlockSpec((tc, w), lambda b, i: (b * nc + i, 0))
    return pl.pallas_call(
        functools.partial(_ret_kernel, chunk_decay),
        out_shape=jax.ShapeDtypeStruct((t, RET_V_WIDTH), BF16),
        grid=(batch, nc),
        in_specs=[
            row(RET_QK_WIDTH),
            pl.BlockSpec((RET_QK_WIDTH, tc), lambda b, i: (0, b * nc + i)),
            row(RET_V_WIDTH),
            row(RET_V_WIDTH),
            _const_spec((1, RET_V_WIDTH)),
            _const_spec((RET_HEADS, RET_CHUNK, RET_CHUNK)),
            _const_spec((RET_HEADS, 1, RET_CHUNK)),
            _const_spec((RET_HEADS, RET_CHUNK, RET_V_DIM)),
        ],
        out_specs=row(RET_V_WIDTH),
        scratch_shapes=[pltpu.VMEM((RET_HEADS, RET_QK_WIDTH, RET_V_DIM), F32)],
        compiler_params=_params(("parallel", "arbitrary")),
        name="ret",
    )(rq, rkt, rv, rg, gn, jnp.asarray(decay_in, F32), jnp.asarray(w_k, F32), jnp.asarray(w_q, F32))


def _memkv_kernel(mem_ref, g_ref, b_ref, w_ref, o_ref):
    m = _layer_norm(mem_ref[...], g_ref[...], b_ref[...])
    o_ref[...] = _dot(m.astype(BF16), w_ref[...]).astype(BF16)


def _memkv(mem2, g, b, w, batch, mlen):
    return pl.pallas_call(
        _memkv_kernel,
        out_shape=jax.ShapeDtypeStruct((batch * mlen, 2 * MEM_WIDTH), BF16),
        grid=(batch,),
        in_specs=[pl.BlockSpec((mlen, D_MODEL), lambda b: (b, 0)), _const_spec((1, D_MODEL)),
                  _const_spec((1, D_MODEL)), _const_spec((D_MODEL, 2 * MEM_WIDTH))],
        out_specs=pl.BlockSpec((mlen, 2 * MEM_WIDTH), lambda b: (b, 0)),
        compiler_params=_params(("parallel",)),
        name="memkv",
    )(mem2, g, b, w)


def _xattn_kernel(q_ref, kv_ref, o_ref):
    scale = MEM_HEAD_DIM ** -0.5
    for h in range(MEM_HEADS):
        cols = slice(h * MEM_HEAD_DIM, (h + 1) * MEM_HEAD_DIM)
        vcols = slice(MEM_WIDTH + h * MEM_HEAD_DIM, MEM_WIDTH + (h + 1) * MEM_HEAD_DIM)
        l = _dot_nt(q_ref[:, cols], kv_ref[:, cols]) * scale
        p = jnp.exp(l - jnp.max(l, axis=-1, keepdims=True))
        den = jnp.sum(p, axis=-1, keepdims=True)
        o_ref[:, cols] = (_dot(p.astype(BF16), kv_ref[:, vcols]) / den).astype(BF16)


def _xattn(mq, mkv, batch, seq, mlen, tq):
    t = mq.shape[0]
    nq = seq // tq
    return pl.pallas_call(
        _xattn_kernel,
        out_shape=jax.ShapeDtypeStruct((t, MEM_WIDTH), BF16),
        grid=(batch, nq),
        in_specs=[pl.BlockSpec((tq, MEM_WIDTH), lambda b, i: (b * nq + i, 0)),
                  pl.BlockSpec((mlen, 2 * MEM_WIDTH), lambda b, i: (b, 0))],
        out_specs=pl.BlockSpec((tq, MEM_WIDTH), lambda b, i: (b * nq + i, 0)),
        compiler_params=_params(("parallel", "parallel")),
        name="xattn",
    )(mq, mkv)


def _merge_kernel(a_ref, r_ref, m_ref, gates_ref, x_ref, wa_ref, wr_ref, wm_ref, wo_ref,
                  g_ref, b_ref, h_ref):
    def sig(j):
        z = gates_ref[:, j * D_MODEL:(j + 1) * D_MODEL].astype(F32)
        return 1.0 / (1.0 + jnp.exp(-z))

    merged = sig(0) * _dot(a_ref[...], wa_ref[...])
    merged += sig(1) * _dot(r_ref[...], wr_ref[...])
    merged += sig(2) * _dot(m_ref[...], wm_ref[...])
    z = ALPHA * x_ref[...] + _dot(merged.astype(BF16), wo_ref[...])
    h_ref[...] = _layer_norm(z, g_ref[...], b_ref[...])


def _merge(br_a, br_r, br_m, gates, x2, wa, wr, wm, wo, g, b, tm):
    t = x2.shape[0]
    row = lambda w: pl.BlockSpec((tm, w), lambda i: (i, 0))
    return pl.pallas_call(
        _merge_kernel,
        out_shape=jax.ShapeDtypeStruct((t, D_MODEL), F32),
        grid=(t // tm,),
        in_specs=[row(ATT_Q_WIDTH), row(RET_V_WIDTH), row(MEM_WIDTH), row(GATE_WIDTH), row(D_MODEL),
                  _const_spec((ATT_Q_WIDTH, D_MODEL)), _const_spec((RET_V_WIDTH, D_MODEL)),
                  _const_spec((MEM_WIDTH, D_MODEL)), _const_spec((D_MODEL, D_MODEL)),
                  _const_spec((1, D_MODEL)), _const_spec((1, D_MODEL))],
        out_specs=row(D_MODEL),
        compiler_params=_params(("parallel",)),
        name="merge",
    )(br_a, br_r, br_m, gates, x2, wa, wr, wm, wo, g, b)


_N_TOP = PEER_TOPK + 1
_TOP_ROWS = 24
_ST_TAU, _ST_M1, _ST_M2, _ST_Z = 0, 1, 2, 3
_ST_ROWS = 8


def _odd_even_merge_sort(n):
    pairs, p = [], 1
    while p < n:
        k = p
        while k >= 1:
            for j in range(k % p, n - k, 2 * k):
                for i in range(min(k, n - j - k)):
                    if (i + j) // (2 * p) == (i + j + k) // (2 * p):
                        pairs.append((i + j, i + j + k))
            k //= 2
        p *= 2
    return tuple(pairs)


_SORT16 = _odd_even_merge_sort(PEER_N_KEYS // 8)


def _sel_kernel(h_ref, wq_ref, keys_ref, ht_ref, s1_ref, s2_ref, st_ref, tops_s, qt_s, sc_s):
    tt = h_ref.shape[0]
    ht = h_ref[...].T.astype(BF16)
    ht_ref[...] = ht
    qt_s[...] = _dot(wq_ref[...], ht).astype(BF16)
    nk = PEER_N_KEYS
    for hp in range(2 * PEER_HEADS):
        s = _dot(keys_ref[hp], qt_s[hp * PEER_HALF:(hp + 1) * PEER_HALF, :])
        hd, p = divmod(hp, 2)
        (s1_ref if p == 0 else s2_ref)[hd * nk:(hd + 1) * nk, :] = s
        sc_s[hp] = s

    row8 = lax.broadcasted_iota(jnp.int32, (8, LANES), 0)
    ngroups = tt // LANES
    ntile = _TOP_ROWS // 8

    def pop_max(depth, steps, extra=()):
        depth, extra, out = list(depth), list(extra), []
        for t in range(steps):
            head = functools.reduce(jnp.maximum, [depth[0]] + extra)
            m = jnp.max(head, axis=0, keepdims=True)
            out.append(m)
            if t == steps - 1:
                break
            hit = depth[0] == m
            for k in range(min(len(depth), steps - 1 - t)):
                nxt = depth[k + 1] if k + 1 < len(depth) else NEG_INF
                depth[k] = jnp.where(hit, nxt, depth[k])
            extra = [jnp.where(e == m, NEG_INF, e) for e in extra]
        return out

    def top_lists(hp, carry):
        for lg in range(ngroups):
            lanes = slice(lg * LANES, (lg + 1) * LANES)
            v = [sc_s[hp, 8 * k:8 * (k + 1), lanes] for k in range(nk // 8)]
            for i, j in _SORT16:
                v[i], v[j] = jnp.maximum(v[i], v[j]), jnp.minimum(v[i], v[j])
            tops = [jnp.full((8, LANES), NEG_INF, F32) for _ in range(ntile)]
            for k, m in enumerate(pop_max(v, _N_TOP)):
                tops[k // 8] = jnp.where(row8 == (k % 8), m, tops[k // 8])
            for j, tile in enumerate(tops):
                tops_s[hp * ntile + j, :, lanes] = tile
        return carry

    lax.fori_loop(0, 2 * PEER_HEADS, top_lists, 0)

    def head_stats(hd, carry):
        for lg in range(ngroups):
            lanes = slice(lg * LANES, (lg + 1) * LANES)
            t1 = [tops_s[2 * hd * ntile + j, :, lanes] for j in range(ntile)]
            t2 = [tops_s[(2 * hd + 1) * ntile + b // 8, b % 8:b % 8 + 1, lanes]
                  for b in range(_N_TOP)]
            depth = [t1[0] + t2[0]]
            for b in range(1, _N_TOP):
                depth.append(jnp.where(row8 < _N_TOP // (b + 1), t1[0] + t2[b], NEG_INF))
            sums = pop_max(depth, _N_TOP, extra=[t + t2[0] for t in t1[1:]])
            z = jnp.zeros((1, LANES), F32)
            for m in sums[:PEER_TOPK]:
                z = z + jnp.exp(m - sums[0])
            tau = 0.5 * (sums[PEER_TOPK - 1] + sums[PEER_TOPK])
            tile = jnp.zeros((_ST_ROWS, LANES), F32)
            for r, val in ((_ST_TAU, tau), (_ST_M1, t1[0][0:1, :]), (_ST_M2, t2[0]), (_ST_Z, z)):
                tile = jnp.where(row8 == r, val, tile)
            st_ref[hd, :, lanes] = tile
        return carry

    lax.fori_loop(0, PEER_HEADS, head_stats, 0)


def _peer_select(h2, wq_t, keys, tt):
    t = h2.shape[0]
    nrow = PEER_HEADS * PEER_N_KEYS
    col = lambda r: pl.BlockSpec((r, tt), lambda i: (0, i))
    return pl.pallas_call(
        _sel_kernel,
        out_shape=(jax.ShapeDtypeStruct((D_MODEL, t), BF16),
                   jax.ShapeDtypeStruct((nrow, t), F32),
                   jax.ShapeDtypeStruct((nrow, t), F32),
                   jax.ShapeDtypeStruct((PEER_HEADS, _ST_ROWS, t), F32),
                   jax.ShapeDtypeStruct((2 * PEER_HEADS * _TOP_ROWS // 8, 8, t), F32)),
        grid=(t // tt,),
        in_specs=[pl.BlockSpec((tt, D_MODEL), lambda i: (i, 0)),
                  _const_spec((2 * PEER_HEADS * PEER_HALF, D_MODEL)),
                  _const_spec((2 * PEER_HEADS, PEER_N_KEYS, PEER_HALF))],
        out_specs=(col(D_MODEL), col(nrow), col(nrow),
                   pl.BlockSpec((PEER_HEADS, _ST_ROWS, tt), lambda i: (0, 0, i)),
                   pl.BlockSpec((2 * PEER_HEADS * _TOP_ROWS // 8, 8, tt), lambda i: (0, 0, i))),
        scratch_shapes=[pltpu.VMEM((2 * PEER_HEADS * PEER_HALF, tt), BF16),
                        pltpu.VMEM((2 * PEER_HEADS, PEER_N_KEYS, tt), F32)],
        compiler_params=_params(("parallel",)),
        name="peer_sel",
    )(h2, wq_t, keys)


_DENSE_KEY_ROWS = 64
_DENSE_IB_GROUP = 4
_DENSE_CODE_ROWS = 32
_NOT_SELECTED = 64.0


def _dense_kernel(nchunk, h_ref, ht_ref, s1_ref, s2_ref, st_ref, tops_ref, u_ref, vt_ref, g_ref, b_ref,
                  y_ref, cnt_s, e2_s, code_s, e1_s, acc_s, ag_s):
    c = pl.program_id(1)
    nk = PEER_N_KEYS
    _, ec, tt = ag_s.shape
    assert ec == 8 * nk
    ngrp = nk // 8
    ntile = _TOP_ROWS // 8

    @pl.when(c < nchunk)
    def _():
        ag_s[c] = _dot(u_ref[...], ht_ref[...]).astype(BF16)

    @pl.when(c == nchunk)
    def _():
        acc_s[...] = jnp.zeros_like(acc_s)
        cr = _DENSE_CODE_ROWS

        def per_head(hd, carry):
            for lg in range(tt // LANES):
                lanes = slice(lg * LANES, (lg + 1) * LANES)
                tau = st_ref[hd, _ST_TAU:_ST_TAU + 1, lanes]
                m1 = st_ref[hd, _ST_M1:_ST_M1 + 1, lanes]
                m2 = st_ref[hd, _ST_M2:_ST_M2 + 1, lanes]
                half_inv_z = 0.5 / st_ref[hd, _ST_Z:_ST_Z + 1, lanes]
                t1 = [tops_ref[2 * hd * ntile + a // 8, a % 8:a % 8 + 1, lanes]
                      for a in range(PEER_TOPK)]
                thr = [tau - t for t in t1]
                for rt in range(nk // cr):
                    rows = pl.ds(pl.multiple_of(hd * nk + rt * cr, cr), cr)
                    grp = pl.ds(hd * ngrp + rt * (cr // 8), cr // 8)
                    s2 = s2_ref[rows, lanes]
                    cnt = jnp.zeros((cr, LANES), F32)
                    for a in range(PEER_TOPK):
                        cnt = jnp.where(s2 >= thr[a], float(a + 1), cnt)
                    cnt_s[rows, lanes] = cnt.astype(BF16)
                    e2_s[rows, lanes] = jnp.exp(s2 - m2).astype(BF16)
                    s1 = s1_ref[rows, lanes]
                    code = jnp.full((cr, LANES), _NOT_SELECTED, F32)
                    for a in reversed(range(PEER_TOPK)):
                        code = jnp.where(s1 >= t1[a], float(a + 1), code)
                    code_s[grp, :, lanes] = code.reshape(cr // 8, 8, LANES)
                    e1_s[grp, :, lanes] = (jnp.exp(s1 - m1) * half_inv_z).reshape(cr // 8, 8, LANES)
            return carry

        lax.fori_loop(0, PEER_HEADS, per_head, 0)

    @pl.when((c >= nchunk) & (c < 2 * nchunk))
    def _():
        ck = c - nchunk
        for g0 in range(0, 8, _DENSE_IB_GROUP):
            for lg in range(tt // LANES):
                lanes = slice(lg * LANES, (lg + 1) * LANES)
                for jq in range(nk // _DENSE_KEY_ROWS):
                    w = [None] * _DENSE_IB_GROUP
                    for hd in range(PEER_HEADS):
                        r0 = hd * nk + jq * _DENSE_KEY_ROWS
                        cnt = cnt_s[r0:r0 + _DENSE_KEY_ROWS, lanes]
                        e2t = e2_s[r0:r0 + _DENSE_KEY_ROWS, lanes]
                        for k in range(_DENSE_IB_GROUP):
                            ib = g0 + k
                            code = code_s[hd * ngrp + ck, ib:ib + 1, lanes].astype(BF16)
                            e1 = e1_s[hd * ngrp + ck, ib:ib + 1, lanes].astype(BF16)
                            contrib = jnp.where(cnt >= code, e2t, jnp.zeros_like(e2t)) * e1
                            w[k] = contrib if w[k] is None else w[k] + contrib
                    for k in range(_DENSE_IB_GROUP):
                        r0 = (g0 + k) * nk + jq * _DENSE_KEY_ROWS
                        av = ag_s[ck, r0:r0 + _DENSE_KEY_ROWS, lanes].astype(F32)
                        ga = av * (1.0 + lax.erf(av * (2.0 ** -0.5)))
                        ag_s[ck, r0:r0 + _DENSE_KEY_ROWS, lanes] = w[k] * ga.astype(BF16)

    @pl.when(c >= 2 * nchunk)
    def _():
        acc_s[...] += _dot(vt_ref[...], ag_s[c - 2 * nchunk])

    @pl.when(c == 3 * nchunk - 1)
    def _():
        z = ALPHA * h_ref[...] + acc_s[...].T
        y_ref[...] = _layer_norm(z, g_ref[...], b_ref[...])


def _peer_dense(h2, ht, s1, s2, st, tops, u, vt, g, b, tt, ec):
    t = h2.shape[0]
    ne = u.shape[0]
    nrow = PEER_HEADS * PEER_N_KEYS
    nchunk = ne // ec
    one = pl.Buffered(1)
    col = lambda r, **kw: pl.BlockSpec((r, tt), lambda i, c: (0, i), **kw)
    return pl.pallas_call(
        functools.partial(_dense_kernel, nchunk),
        out_shape=jax.ShapeDtypeStruct((t, D_MODEL), F32),
        grid=(t // tt, 3 * nchunk),
        in_specs=[pl.BlockSpec((tt, D_MODEL), lambda i, c: (i, 0), pipeline_mode=one),
                  col(D_MODEL), col(nrow, pipeline_mode=one), col(nrow, pipeline_mode=one),
                  pl.BlockSpec((PEER_HEADS, _ST_ROWS, tt), lambda i, c: (0, 0, i)),
                  pl.BlockSpec((tops.shape[0], 8, tt), lambda i, c: (0, 0, i), pipeline_mode=one),
                  pl.BlockSpec((ec, D_MODEL), lambda i, c: (jnp.minimum(c, nchunk - 1), 0)),
                  pl.BlockSpec((D_MODEL, ec),
                               lambda i, c: (0, jnp.clip(c - 2 * nchunk, 0, nchunk - 1))),
                  pl.BlockSpec((1, D_MODEL), lambda i, c: (0, 0)),
                  pl.BlockSpec((1, D_MODEL), lambda i, c: (0, 0))],
        out_specs=pl.BlockSpec((tt, D_MODEL), lambda i, c: (i, 0)),
        scratch_shapes=[pltpu.VMEM((nrow, tt), BF16), pltpu.VMEM((nrow, tt), BF16),
                        pltpu.VMEM((nrow // 8, 8, tt), F32),
                        pltpu.VMEM((nrow // 8, 8, tt), F32), pltpu.VMEM((D_MODEL, tt), F32),
                        pltpu.VMEM((nchunk, ec, tt), BF16)],
        compiler_params=_params(("parallel", "arbitrary")),
        name="peer_dense",
    )(h2, ht, s1, s2, st, tops, u, vt, g, b)


def _prep_w_in(w_in, b_in):
    sizes = (ATT_Q_WIDTH, ATT_KV_WIDTH, ATT_KV_WIDTH, RET_QK_WIDTH, RET_QK_WIDTH, RET_V_WIDTH,
             RET_V_WIDTH, MEM_WIDTH, GATE_WIDTH)
    offs = np.cumsum(sizes)[:-1].tolist()
    wb = jnp.concatenate([w_in, b_in[None, :]], axis=0)
    aq, ak, av, rq, rk, rv, rg, mq, gates = jnp.split(wb, offs, axis=1)
    rows = wb.shape[0]
    rep = ATT_HEADS // ATT_KV_HEADS
    half = RET_QK_DIM // 2
    aq = aq.reshape(rows, ATT_KV_HEADS, rep, ATT_HEAD_DIM).transpose(0, 2, 1, 3).reshape(rows, -1)

    def halves_first(w):
        return w.reshape(rows, RET_HEADS, 2, half).transpose(0, 2, 1, 3).reshape(rows, -1)

    main = jnp.concatenate([aq, ak, av, halves_first(rq), rv, rg, mq, gates], axis=1)
    assert main.shape[1] == _W_MAIN
    rk = halves_first(rk)
    return (main[:-1].astype(BF16), main[-1:], rk[:-1].T.astype(BF16), rk[-1][:, None])


def _prep_w_branch_attn(w):
    rep = ATT_HEADS // ATT_KV_HEADS
    return (w.reshape(ATT_KV_HEADS, rep, ATT_HEAD_DIM, -1).transpose(1, 0, 2, 3)
            .reshape(ATT_Q_WIDTH, -1).astype(BF16))


def _rotation_tables(seq):
    half = RET_QK_DIM // 2
    pos = jnp.arange(seq, dtype=F32)
    theta = 1.0 / (10000.0 ** jnp.linspace(0.0, 1.0, half, dtype=F32))
    ang = pos[:, None] * theta[None, :]
    cos = jnp.tile(jnp.cos(ang), (1, LANES // half))
    sin = jnp.tile(jnp.sin(ang), (1, LANES // half))
    return cos, sin, cos.T, sin.T


def _layer(x, mem, w_in, b_in, attn_sinks, ret_gn_g, mem_ln_g, mem_ln_b, w_mem_kv,
           w_branch_attn, w_branch_ret, w_branch_mem, w_out, ln1_g, ln1_b,
           w_peer_q, peer_sub_keys, peer_u, peer_v, ln2_g, ln2_b):
    batch, seq, d = x.shape
    mlen = mem.shape[1]
    t = batch * seq
    tile = min(512, seq)
    x2 = x.reshape(t, d)
    row = lambda v: v.reshape(1, -1).astype(F32)

    w_main, b_main, w_rkt, b_rkt = _prep_w_in(w_in, b_in)
    cos, sin, cos_t, sin_t = _rotation_tables(seq)
    aq, akv, rq, rv, rg, mq, gates, rkt = _proj(
        x2, w_main, b_main, w_rkt, b_rkt, cos, sin, cos_t, sin_t, seq, tile)

    br_a = _attention(aq, akv, attn_sinks.astype(F32), batch, seq, tile)
    br_r = _retention(rq, rkt, rv, rg, row(ret_gn_g), batch, seq, tile)
    mkv = _memkv(mem.reshape(batch * mlen, d), row(mem_ln_g), row(mem_ln_b),
                 w_mem_kv.astype(BF16), batch, mlen)
    br_m = _xattn(mq, mkv, batch, seq, mlen, tile)

    h2 = _merge(br_a, br_r, br_m, gates, x2,
                _prep_w_branch_attn(w_branch_attn), w_branch_ret.astype(BF16),
                w_branch_mem.astype(BF16), w_out.astype(BF16), row(ln1_g), row(ln1_b), tile)

    wq_t = w_peer_q.T.astype(BF16)
    keys = peer_sub_keys.reshape(2 * PEER_HEADS, PEER_N_KEYS, PEER_HALF).astype(BF16)
    ht, s1, s2, st, tops = _peer_select(h2, wq_t, keys, tile)
    y2 = _peer_dense(h2, ht, s1, s2, st, tops, peer_u.astype(BF16), peer_v.T.astype(BF16),
                     row(ln2_g), row(ln2_b), tile, 8 * PEER_N_KEYS)
    return y2.reshape(batch, seq, d)


def kernel(x, mem, w_in, b_in, attn_sinks, ret_gn_g, mem_ln_g, mem_ln_b, w_mem_kv, w_branch_attn,
           w_branch_ret, w_branch_mem, w_out, ln1_g, ln1_b, w_peer_q, peer_sub_keys, peer_u, peer_v,
           ln2_g, ln2_b):
    for l in range(DEPTH):
        x = _layer(x, mem, w_in[l], b_in[l], attn_sinks[l], ret_gn_g[l], mem_ln_g[l], mem_ln_b[l],
                   w_mem_kv[l], w_branch_attn[l], w_branch_ret[l], w_branch_mem[l], w_out[l],
                   ln1_g[l], ln1_b[l], w_peer_q[l], peer_sub_keys[l], peer_u[l], peer_v[l],
                   ln2_g[l], ln2_b[l])
    return x
```

```python
import functools

import numpy as np
import jax
import jax.numpy as jnp
from jax import lax
from jax.experimental import pallas as pl
from jax.experimental.pallas import tpu as pltpu

F32 = jnp.float32
BF16 = jnp.bfloat16

D_MODEL = 1024
DEPTH = 1
ATT_HEADS, ATT_KV_HEADS, ATT_HEAD_DIM, ATT_BLOCK = 8, 2, 64, 128
RET_HEADS, RET_QK_DIM, RET_V_DIM, RET_CHUNK = 4, 64, 128, 128
MEM_HEADS, MEM_HEAD_DIM = 4, 128
PEER_HEADS, PEER_N_KEYS, PEER_TOPK, PEER_HALF = 8, 128, 16, 128
ALPHA = (2.0 * DEPTH) ** 0.25
LN_EPS = 1e-5
NEG_INF = float("-inf")

ATT_Q_WIDTH = ATT_HEADS * ATT_HEAD_DIM
ATT_KV_WIDTH = ATT_KV_HEADS * ATT_HEAD_DIM
RET_QK_WIDTH = RET_HEADS * RET_QK_DIM
RET_V_WIDTH = RET_HEADS * RET_V_DIM
MEM_WIDTH = MEM_HEADS * MEM_HEAD_DIM
GATE_WIDTH = 3 * D_MODEL

VMEM_LIMIT_BYTES = 56 * 1024 * 1024
LANES = 128

_C_AQ = (0, 512)
_C_AKV = (512, 768)
_C_RQ = (768, 1024)
_C_RV = (1024, 1536)
_C_RG = (1536, 2048)
_C_MQ = (2048, 2560)
_C_GATES = (2560, 5632)
_W_MAIN = 5632


def _dot(a, b):
    return jnp.dot(a, b, preferred_element_type=F32)


def _dot_nt(a, b):
    return lax.dot_general(a, b, (((1,), (1,)), ((), ())), preferred_element_type=F32)


def _layer_norm(z, g, b):
    mu = jnp.mean(z, axis=-1, keepdims=True)
    zc = z - mu
    var = jnp.mean(zc * zc, axis=-1, keepdims=True)
    return zc * lax.rsqrt(var + LN_EPS) * g + b


def _const_spec(shape):
    nd = len(shape)
    return pl.BlockSpec(shape, lambda *_: (0,) * nd, pipeline_mode=pl.Buffered(1))


def _params(sem, flags=None):
    return pltpu.CompilerParams(dimension_semantics=sem, vmem_limit_bytes=VMEM_LIMIT_BYTES,
                                flags=flags)


def _proj_kernel(x_ref, w_ref, b_ref, wk_ref, bk_ref, cos_ref, sin_ref, cost_ref, sint_ref,
                 aq_ref, akv_ref, rq_ref, rv_ref, rg_ref, mq_ref, gates_ref, rkt_ref):
    xb = x_ref[...].astype(BF16)

    def mm(lo, hi):
        return _dot(xb, w_ref[:, lo:hi]) + b_ref[:, lo:hi]

    aq_ref[...] = mm(*_C_AQ).astype(BF16)
    akv_ref[...] = mm(*_C_AKV).astype(BF16)
    r = mm(*_C_RQ)
    t1, t2 = r[:, :LANES], r[:, LANES:]
    c, s = cos_ref[...], sin_ref[...]
    rq_ref[:, :LANES] = (t1 * c - t2 * s).astype(BF16)
    rq_ref[:, LANES:] = (t1 * s + t2 * c).astype(BF16)
    rv_ref[...] = mm(*_C_RV).astype(BF16)
    rg_ref[...] = mm(*_C_RG).astype(BF16)
    mq_ref[...] = mm(*_C_MQ).astype(BF16)
    for j in range(GATE_WIDTH // 512):
        lo = _C_GATES[0] + j * 512
        gates_ref[:, j * 512:(j + 1) * 512] = mm(lo, lo + 512).astype(BF16)
    kt = _dot_nt(wk_ref[...], xb) + bk_ref[...]
    k1, k2 = kt[:LANES], kt[LANES:]
    ct, st = cost_ref[...], sint_ref[...]
    scale = RET_QK_DIM ** -0.5
    rkt_ref[:LANES, :] = ((k1 * ct - k2 * st) * scale).astype(BF16)
    rkt_ref[LANES:, :] = ((k1 * st + k2 * ct) * scale).astype(BF16)


def _proj(x2, w_main, b_main, w_rkt, b_rkt, cos, sin, cos_t, sin_t, seq, tm):
    t = x2.shape[0]
    nseq = seq // tm
    row = lambda w: pl.BlockSpec((tm, w), lambda i: (i, 0))
    out_shape = (
        jax.ShapeDtypeStruct((t, ATT_Q_WIDTH), BF16),
        jax.ShapeDtypeStruct((t, 2 * ATT_KV_WIDTH), BF16),
        jax.ShapeDtypeStruct((t, RET_QK_WIDTH), BF16),
        jax.ShapeDtypeStruct((t, RET_V_WIDTH), BF16),
        jax.ShapeDtypeStruct((t, RET_V_WIDTH), BF16),
        jax.ShapeDtypeStruct((t, MEM_WIDTH), BF16),
        jax.ShapeDtypeStruct((t, GATE_WIDTH), BF16),
        jax.ShapeDtypeStruct((RET_QK_WIDTH, t), BF16),
    )
    return pl.pallas_call(
        _proj_kernel,
        out_shape=out_shape,
        grid=(t // tm,),
        in_specs=[
            row(D_MODEL),
            _const_spec((D_MODEL, _W_MAIN)),
            _const_spec((1, _W_MAIN)),
            _const_spec((RET_QK_WIDTH, D_MODEL)),
            _const_spec((RET_QK_WIDTH, 1)),
            pl.BlockSpec((tm, LANES), lambda i: (i % nseq, 0)),
            pl.BlockSpec((tm, LANES), lambda i: (i % nseq, 0)),
            pl.BlockSpec((LANES, tm), lambda i: (0, i % nseq)),
            pl.BlockSpec((LANES, tm), lambda i: (0, i % nseq)),
        ],
        out_specs=(row(ATT_Q_WIDTH), row(2 * ATT_KV_WIDTH), row(RET_QK_WIDTH), row(RET_V_WIDTH),
                   row(RET_V_WIDTH), row(MEM_WIDTH), row(GATE_WIDTH),
                   pl.BlockSpec((RET_QK_WIDTH, tm), lambda i: (0, i))),
        compiler_params=_params(("parallel",)),
        name="proj",
    )(x2, w_main, b_main, w_rkt, b_rkt, cos, sin, cos_t, sin_t)


def _attn_kernel(sink_ref, q_ref, kv_ref, kvp_ref, o_ref, kvs_ref):
    i = pl.program_id(1)
    tq = q_ref.shape[0]
    blk = ATT_BLOCK
    kvs_ref[0:blk, :] = kvp_ref[...]
    kvs_ref[blk:blk + tq, :] = kv_ref[...]
    qi = lax.broadcasted_iota(jnp.int32, (blk, 2 * blk), 0)
    kj = lax.broadcasted_iota(jnp.int32, (blk, 2 * blk), 1)
    base_mask = ((kj < blk) & (kj > qi)) | ((kj >= blk) & ((kj - blk) <= qi))
    first_key = jnp.where(i > 0, 0, blk)
    lane = lax.broadcasted_iota(jnp.int32, (blk, LANES), 1)
    low = lane < ATT_HEAD_DIM
    qmask = (low.astype(F32).astype(BF16), jnp.logical_not(low).astype(F32).astype(BF16))
    rep = ATT_HEADS // ATT_KV_HEADS
    scale = ATT_HEAD_DIM ** -0.5
    for b in range(tq // blk):
        mask = (base_mask & (kj >= first_key)) if b == 0 else base_mask
        kv = kvs_ref[b * blk:(b + 2) * blk, :]
        k, v = kv[:, :LANES], kv[:, LANES:]
        for r in range(rep):
            qc = q_ref[b * blk:(b + 1) * blk, r * LANES:(r + 1) * LANES]
            outs = []
            for g in range(ATT_KV_HEADS):
                qm = qc * qmask[g]
                l = _dot_nt(qm, k) * scale
                l = jnp.where(mask, l, NEG_INF)
                sink = sink_ref[g * rep + r]
                m = jnp.maximum(jnp.max(l, axis=-1, keepdims=True), sink)
                p = jnp.exp(l - m)
                den = jnp.sum(p, axis=-1, keepdims=True) + jnp.exp(sink - m)
                outs.append(_dot(p.astype(BF16), v) / den)
            o_ref[b * blk:(b + 1) * blk, r * LANES:(r + 1) * LANES] = (
                jnp.where(low, outs[0], outs[1]).astype(BF16))


def _ret_kernel(chunk_decay, q_ref, kt_ref, v_ref, g_ref, gn_ref, din_ref, wk_ref, wq_ref,
                o_ref, state_ref):
    i = pl.program_id(1)
    tc = q_ref.shape[0]
    ck = RET_CHUNK

    @pl.when(i == 0)
    def _():
        state_ref[...] = jnp.zeros_like(state_ref)

    half = RET_QK_DIM // 2
    rowi = lax.broadcasted_iota(jnp.int32, (RET_QK_WIDTH, ck), 0)
    for c in range(tc // ck):
        rows = slice(c * ck, (c + 1) * ck)
        q = q_ref[rows, :]
        kt = kt_ref[:, rows].astype(F32)
        for h in range(RET_HEADS):
            cols = slice(h * RET_V_DIM, (h + 1) * RET_V_DIM)
            kth = jnp.where((rowi % LANES) // half == h, kt, 0.0)
            vh = v_ref[rows, cols]
            inner = _dot(q, kth.astype(BF16)) * din_ref[h]
            o_in = _dot(inner.astype(BF16), vh)
            st = state_ref[h]
            o_x = _dot(q, st.astype(BF16)) * wq_ref[h]
            state_ref[h] = chunk_decay[h] * st + _dot((kth * wk_ref[h]).astype(BF16), vh)
            o = o_in + o_x
            mu = jnp.mean(o, axis=-1, keepdims=True)
            oc = o - mu
            var = jnp.mean(oc * oc, axis=-1, keepdims=True)
            y = oc * lax.rsqrt(var + LN_EPS) * gn_ref[:, cols]
            gate = g_ref[rows, cols].astype(F32)
            o_ref[rows, cols] = (y * (gate / (1.0 + jnp.exp(-gate)))).astype(BF16)


def _memkv_kernel(mem_ref, g_ref, b_ref, w_ref, o_ref):
    m = _layer_norm(mem_ref[...], g_ref[...], b_ref[...])
    o_ref[...] = _dot(m.astype(BF16), w_ref[...]).astype(BF16)


def _memkv(mem2, g, b, w, batch, mlen):
    return pl.pallas_call(
        _memkv_kernel,
        out_shape=jax.ShapeDtypeStruct((batch * mlen, 2 * MEM_WIDTH), BF16),
        grid=(batch,),
        in_specs=[pl.BlockSpec((mlen, D_MODEL), lambda b: (b, 0)), _const_spec((1, D_MODEL)),
                  _const_spec((1, D_MODEL)), _const_spec((D_MODEL, 2 * MEM_WIDTH))],
        out_specs=pl.BlockSpec((mlen, 2 * MEM_WIDTH), lambda b: (b, 0)),
        compiler_params=_params(("parallel",)),
        name="memkv",
    )(mem2, g, b, w)


def _xattn_kernel(q_ref, kv_ref, o_ref):
    scale = MEM_HEAD_DIM ** -0.5
    for h in range(MEM_HEADS):
        cols = slice(h * MEM_HEAD_DIM, (h + 1) * MEM_HEAD_DIM)
        vcols = slice(MEM_WIDTH + h * MEM_HEAD_DIM, MEM_WIDTH + (h + 1) * MEM_HEAD_DIM)
        l = _dot_nt(q_ref[:, cols], kv_ref[:, cols]) * scale
        p = jnp.exp(l - jnp.max(l, axis=-1, keepdims=True))
        den = jnp.sum(p, axis=-1, keepdims=True)
        o_ref[:, cols] = (_dot(p.astype(BF16), kv_ref[:, vcols]) / den).astype(BF16)


def _mixers_kernel(chunk_decay, sink_ref, aq_ref, akv_ref, akvp_ref, rq_ref, rkt_ref, rv_ref, rg_ref,
                   gn_ref, din_ref, wk_ref, wq_ref, mq_ref, mkv_ref, oa_ref, or_ref, om_ref,
                   kvs_ref, state_ref):
    _attn_kernel(sink_ref, aq_ref, akv_ref, akvp_ref, oa_ref, kvs_ref)
    _ret_kernel(chunk_decay, rq_ref, rkt_ref, rv_ref, rg_ref, gn_ref, din_ref, wk_ref, wq_ref,
                or_ref, state_ref)
    _xattn_kernel(mq_ref, mkv_ref, om_ref)


def _mixers(aq, akv, sinks, rq, rkt, rv, rg, gn, mq, mkv, batch, seq, mlen, tq):
    t = aq.shape[0]
    nq = seq // tq
    per = tq // ATT_BLOCK
    log_gamma = np.log(1.0 - 2.0 ** (-5.0 - np.arange(RET_HEADS, dtype=np.float64)))
    idx = np.arange(RET_CHUNK, dtype=np.float64)
    diff = idx[:, None] - idx[None, :]
    decay_in = np.where(diff >= 0, np.exp(log_gamma[:, None, None] * np.maximum(diff, 0.0)), 0.0)
    w_k = np.exp(log_gamma[:, None] * (RET_CHUNK - 1.0 - idx)[None, :])[:, None, :]
    w_q = np.broadcast_to(np.exp(log_gamma[:, None] * (idx + 1.0)[None, :])[:, :, None],
                          (RET_HEADS, RET_CHUNK, RET_V_DIM))
    chunk_decay = tuple(float(v) for v in np.exp(log_gamma * RET_CHUNK))
    row = lambda w: pl.BlockSpec((tq, w), lambda b, i: (b * nq + i, 0))
    out = lambda w: jax.ShapeDtypeStruct((t, w), BF16)
    return pl.pallas_call(
        functools.partial(_mixers_kernel, chunk_decay),
        out_shape=(out(ATT_Q_WIDTH), out(RET_V_WIDTH), out(MEM_WIDTH)),
        grid=(batch, nq),
        in_specs=[
            pl.BlockSpec(memory_space=pltpu.SMEM),
            row(ATT_Q_WIDTH),
            row(2 * ATT_KV_WIDTH),
            pl.BlockSpec((ATT_BLOCK, 2 * ATT_KV_WIDTH),
                         lambda b, i: (jnp.maximum((b * nq + i) * per - 1, 0), 0)),
            row(RET_QK_WIDTH),
            pl.BlockSpec((RET_QK_WIDTH, tq), lambda b, i: (0, b * nq + i)),
            row(RET_V_WIDTH),
            row(RET_V_WIDTH),
            _const_spec((1, RET_V_WIDTH)),
            _const_spec((RET_HEADS, RET_CHUNK, RET_CHUNK)),
            _const_spec((RET_HEADS, 1, RET_CHUNK)),
            _const_spec((RET_HEADS, RET_CHUNK, RET_V_DIM)),
            row(MEM_WIDTH),
            pl.BlockSpec((mlen, 2 * MEM_WIDTH), lambda b, i: (b, 0)),
        ],
        out_specs=(row(ATT_Q_WIDTH), row(RET_V_WIDTH), row(MEM_WIDTH)),
        scratch_shapes=[pltpu.VMEM((tq + ATT_BLOCK, 2 * ATT_KV_WIDTH), BF16),
                        pltpu.VMEM((RET_HEADS, RET_QK_WIDTH, RET_V_DIM), F32)],
        compiler_params=_params(("parallel", "arbitrary")),
        name="mixers",
    )(sinks, aq, akv, akv, rq, rkt, rv, rg, gn, jnp.asarray(decay_in, F32), jnp.asarray(w_k, F32),
      jnp.asarray(w_q, F32), mq, mkv)


def _merge_kernel(a_ref, r_ref, m_ref, gates_ref, x_ref, wa_ref, wr_ref, wm_ref, wo_ref,
                  g_ref, b_ref, h_ref):
    def sig(j):
        z = gates_ref[:, j * D_MODEL:(j + 1) * D_MODEL].astype(F32)
        return 1.0 / (1.0 + jnp.exp(-z))

    merged = sig(0) * _dot(a_ref[...], wa_ref[...])
    merged += sig(1) * _dot(r_ref[...], wr_ref[...])
    merged += sig(2) * _dot(m_ref[...], wm_ref[...])
    z = ALPHA * x_ref[...] + _dot(merged.astype(BF16), wo_ref[...])
    h_ref[...] = _layer_norm(z, g_ref[...], b_ref[...])


def _merge(br_a, br_r, br_m, gates, x2, wa, wr, wm, wo, g, b, tm):
    t = x2.shape[0]
    row = lambda w: pl.BlockSpec((tm, w), lambda i: (i, 0))
    return pl.pallas_call(
        _merge_kernel,
        out_shape=jax.ShapeDtypeStruct((t, D_MODEL), F32),
        grid=(t // tm,),
        in_specs=[row(ATT_Q_WIDTH), row(RET_V_WIDTH), row(MEM_WIDTH), row(GATE_WIDTH), row(D_MODEL),
                  _const_spec((ATT_Q_WIDTH, D_MODEL)), _const_spec((RET_V_WIDTH, D_MODEL)),
                  _const_spec((MEM_WIDTH, D_MODEL)), _const_spec((D_MODEL, D_MODEL)),
                  _const_spec((1, D_MODEL)), _const_spec((1, D_MODEL))],
        out_specs=row(D_MODEL),
        compiler_params=_params(("parallel",)),
        name="merge",
    )(br_a, br_r, br_m, gates, x2, wa, wr, wm, wo, g, b)


_N_TOP = PEER_TOPK + 1
_TOP_ROWS = 24
_ST_TAU, _ST_M1, _ST_M2, _ST_Z = 0, 1, 2, 3
_ST_ROWS = 8


def _odd_even_merge_sort(n):
    pairs, p = [], 1
    while p < n:
        k = p
        while k >= 1:
            for j in range(k % p, n - k, 2 * k):
                for i in range(min(k, n - j - k)):
                    if (i + j) // (2 * p) == (i + j + k) // (2 * p):
                        pairs.append((i + j, i + j + k))
            k //= 2
        p *= 2
    return tuple(pairs)


_SORT16 = _odd_even_merge_sort(PEER_N_KEYS // 8)


def _sel_kernel(h_ref, wq_ref, keys_ref, ht_ref, s1_ref, s2_ref, st_ref, qt_s, sc_s, tops_s):
    tt = h_ref.shape[0]
    ht = h_ref[...].T.astype(BF16)
    ht_ref[...] = ht
    qt_s[...] = _dot(wq_ref[...], ht).astype(BF16)
    nk = PEER_N_KEYS
    for hp in range(2 * PEER_HEADS):
        s = _dot(keys_ref[hp], qt_s[hp * PEER_HALF:(hp + 1) * PEER_HALF, :])
        hd, p = divmod(hp, 2)
        (s1_ref if p == 0 else s2_ref)[hd * nk:(hd + 1) * nk, :] = s
        sc_s[hp] = s

    row8 = lax.broadcasted_iota(jnp.int32, (8, LANES), 0)
    ngroups = tt // LANES
    ntile = _TOP_ROWS // 8

    def pop_max(depth, steps, extra=()):
        depth, extra, out = list(depth), list(extra), []
        for t in range(steps):
            head = functools.reduce(jnp.maximum, [depth[0]] + extra)
            m = jnp.max(head, axis=0, keepdims=True)
            out.append(m)
            if t == steps - 1:
                break
            hit = depth[0] == m
            for k in range(min(len(depth), steps - 1 - t)):
                nxt = depth[k + 1] if k + 1 < len(depth) else NEG_INF
                depth[k] = jnp.where(hit, nxt, depth[k])
            extra = [jnp.where(e == m, NEG_INF, e) for e in extra]
        return out

    def top_lists(hp, carry):
        for lg in range(ngroups):
            lanes = slice(lg * LANES, (lg + 1) * LANES)
            v = [sc_s[hp, 8 * k:8 * (k + 1), lanes] for k in range(nk // 8)]
            for i, j in _SORT16:
                v[i], v[j] = jnp.maximum(v[i], v[j]), jnp.minimum(v[i], v[j])
            tops = [jnp.full((8, LANES), NEG_INF, F32) for _ in range(ntile)]
            for k, m in enumerate(pop_max(v, _N_TOP)):
                tops[k // 8] = jnp.where(row8 == (k % 8), m, tops[k // 8])
            for j, tile in enumerate(tops):
                tops_s[hp * ntile + j, :, lanes] = tile
        return carry

    lax.fori_loop(0, 2 * PEER_HEADS, top_lists, 0)

    def head_stats(hd, carry):
        for lg in range(ngroups):
            lanes = slice(lg * LANES, (lg + 1) * LANES)
            t1 = [tops_s[2 * hd * ntile + j, :, lanes] for j in range(ntile)]
            t2 = [tops_s[(2 * hd + 1) * ntile + b // 8, b % 8:b % 8 + 1, lanes]
                  for b in range(_N_TOP)]
            depth = [t1[0] + t2[0]]
            for b in range(1, _N_TOP):
                depth.append(jnp.where(row8 < _N_TOP // (b + 1), t1[0] + t2[b], NEG_INF))
            sums = pop_max(depth, _N_TOP, extra=[t + t2[0] for t in t1[1:]])
            z = jnp.zeros((1, LANES), F32)
            for m in sums[:PEER_TOPK]:
                z = z + jnp.exp(m - sums[0])
            tau = 0.5 * (sums[PEER_TOPK - 1] + sums[PEER_TOPK])
            tile = jnp.zeros((_ST_ROWS, LANES), F32)
            for r, val in ((_ST_TAU, tau), (_ST_M1, t1[0][0:1, :]), (_ST_M2, t2[0]), (_ST_Z, z)):
                tile = jnp.where(row8 == r, val, tile)
            st_ref[hd, :, lanes] = tile
        return carry

    lax.fori_loop(0, PEER_HEADS, head_stats, 0)


def _peer_select(h2, wq_t, keys, tt):
    t = h2.shape[0]
    nrow = PEER_HEADS * PEER_N_KEYS
    col = lambda r: pl.BlockSpec((r, tt), lambda i: (0, i))
    return pl.pallas_call(
        _sel_kernel,
        out_shape=(jax.ShapeDtypeStruct((D_MODEL, t), BF16),
                   jax.ShapeDtypeStruct((nrow, t), F32),
                   jax.ShapeDtypeStruct((nrow, t), F32),
                   jax.ShapeDtypeStruct((PEER_HEADS, _ST_ROWS, t), F32)),
        grid=(t // tt,),
        in_specs=[pl.BlockSpec((tt, D_MODEL), lambda i: (i, 0)),
                  _const_spec((2 * PEER_HEADS * PEER_HALF, D_MODEL)),
                  _const_spec((2 * PEER_HEADS, PEER_N_KEYS, PEER_HALF))],
        out_specs=(col(D_MODEL), col(nrow), col(nrow),
                   pl.BlockSpec((PEER_HEADS, _ST_ROWS, tt), lambda i: (0, 0, i))),
        scratch_shapes=[pltpu.VMEM((2 * PEER_HEADS * PEER_HALF, tt), BF16),
                        pltpu.VMEM((2 * PEER_HEADS, PEER_N_KEYS, tt), F32),
                        pltpu.VMEM((2 * PEER_HEADS * _TOP_ROWS // 8, 8, tt), F32)],
        compiler_params=_params(("parallel",)),
        name="peer_sel",
    )(h2, wq_t, keys)


_DENSE_KEY_ROWS = 32
_DENSE_IB_GROUP = 4
_DENSE_MXU_ROW_SPLIT = 1
_DENSE_MXU_COL_SPLIT = 1


def _dense_kernel(nchunk, h_ref, ht_ref, s1_ref, s2_ref, st_ref, u_ref, vt_ref, g_ref, b_ref, y_ref,
                  e2_s, thr_s, e1_s, acc_s, a0_s, a1_s, gs0_s, gs1_s):
    c = pl.program_id(1)
    nk = PEER_N_KEYS
    a_s = (a0_s, a1_s)
    gs_s = (gs0_s, gs1_s)
    ec, tt = gs0_s.shape
    assert ec == 8 * nk and ec == D_MODEL
    ngrp = nk // 8

    @pl.when(c == 0)
    def _():
        gs_s[1][...] = jnp.zeros((ec, tt), BF16)
        for hd in range(PEER_HEADS):
            rows = slice(hd * nk, (hd + 1) * nk)
            tau = st_ref[hd, _ST_TAU:_ST_TAU + 1, :]
            m1 = st_ref[hd, _ST_M1:_ST_M1 + 1, :]
            m2 = st_ref[hd, _ST_M2:_ST_M2 + 1, :]
            z = st_ref[hd, _ST_Z:_ST_Z + 1, :]
            e2_s[rows, :] = jnp.exp(s2_ref[rows, :] - m2)
            thr_s[hd * ngrp:(hd + 1) * ngrp] = (tau - s1_ref[rows, :]).reshape(ngrp, 8, tt)
            e1_s[hd * ngrp:(hd + 1) * ngrp] = (
                (2.0 ** -0.5) * jnp.exp(s1_ref[rows, :] - m1) / z).reshape(ngrp, 8, tt)
        acc_s[...] = jnp.zeros_like(acc_s)

    cb = jnp.clip(c - 1, 0, nchunk - 1)

    def stages(cur, prv, do_act=True, do_mask=True, do_out=True):
        def act_piece(rows, cols):
            a_s[cur][rows, cols] = _dot(u_ref[rows, :], ht_ref[:, cols])

        def out_piece(rows, cols):
            acc_s[rows, cols] += _dot(vt_ref[rows, :], gs_s[cur][:, cols])

        def mask_group(g0, lg, jq):
            lanes = slice(lg * LANES, (lg + 1) * LANES)
            w = [None] * _DENSE_IB_GROUP
            for hd in range(PEER_HEADS):
                r0 = hd * nk + jq * _DENSE_KEY_ROWS
                s2t = s2_ref[r0:r0 + _DENSE_KEY_ROWS, lanes]
                e2t = e2_s[r0:r0 + _DENSE_KEY_ROWS, lanes]
                for k in range(_DENSE_IB_GROUP):
                    ib = g0 + k
                    thr = thr_s[hd * ngrp + cb, ib:ib + 1, lanes]
                    e1 = e1_s[hd * ngrp + cb, ib:ib + 1, lanes]
                    contrib = jnp.where(s2t >= thr, e2t, 0.0) * e1
                    w[k] = contrib if w[k] is None else w[k] + contrib
            for k in range(_DENSE_IB_GROUP):
                r0 = (g0 + k) * nk + jq * _DENSE_KEY_ROWS
                av = a_s[prv][r0:r0 + _DENSE_KEY_ROWS, lanes]
                ga = av * (1.0 + lax.erf(av))
                gs_s[prv][r0:r0 + _DENSE_KEY_ROWS, lanes] = (w[k] * ga).astype(BF16)

        mr, mc = ec // _DENSE_MXU_ROW_SPLIT, tt // _DENSE_MXU_COL_SPLIT
        blocks = [(slice(i * mr, (i + 1) * mr), slice(j * mc, (j + 1) * mc))
                  for j in range(_DENSE_MXU_COL_SPLIT) for i in range(_DENSE_MXU_ROW_SPLIT)]
        pieces = []
        for blk in blocks:
            if do_act:
                pieces.append(functools.partial(act_piece, *blk))
            if do_out:
                pieces.append(functools.partial(out_piece, *blk))
        groups = []
        if do_mask:
            groups = [functools.partial(mask_group, g0, lg, jq)
                      for g0 in range(0, 8, _DENSE_IB_GROUP)
                      for lg in range(tt // LANES)
                      for jq in range(nk // _DENSE_KEY_ROWS)]
        per = -(-len(groups) // len(pieces))
        for i, piece in enumerate(pieces):
            piece()
            for grp in groups[i * per:(i + 1) * per]:
                grp()
        for grp in groups[len(pieces) * per:]:
            grp()

    last = nchunk + 1
    pl.when(c == 0)(functools.partial(stages, 0, 1, do_mask=False, do_out=False))
    pl.when((c % 2 == 0) & (c > 0) & (c < last))(functools.partial(stages, 0, 1))
    pl.when((c % 2 == 1) & (c < last))(functools.partial(stages, 1, 0))
    pl.when(c == last)(functools.partial(stages, last % 2, 1 - last % 2, do_act=False, do_mask=False))

    @pl.when(c == last)
    def _():
        z = ALPHA * h_ref[...] + acc_s[...].T
        y_ref[...] = _layer_norm(z, g_ref[...], b_ref[...])


def _peer_dense(h2, ht, s1, s2, st, u, vt, g, b, tt, ec):
    t = h2.shape[0]
    ne = u.shape[0]
    nrow = PEER_HEADS * PEER_N_KEYS
    col = lambda r: pl.BlockSpec((r, tt), lambda i, c: (0, i))
    nchunk = ne // ec
    return pl.pallas_call(
        functools.partial(_dense_kernel, nchunk),
        out_shape=jax.ShapeDtypeStruct((t, D_MODEL), F32),
        grid=(t // tt, nchunk + 2),
        in_specs=[pl.BlockSpec((tt, D_MODEL), lambda i, c: (i, 0)),
                  col(D_MODEL), col(nrow), col(nrow),
                  pl.BlockSpec((PEER_HEADS, _ST_ROWS, tt), lambda i, c: (0, 0, i)),
                  pl.BlockSpec((ec, D_MODEL), lambda i, c: (jnp.minimum(c, nchunk - 1), 0)),
                  pl.BlockSpec((D_MODEL, ec), lambda i, c: (0, jnp.clip(c - 2, 0, nchunk - 1))),
                  pl.BlockSpec((1, D_MODEL), lambda i, c: (0, 0)),
                  pl.BlockSpec((1, D_MODEL), lambda i, c: (0, 0))],
        out_specs=pl.BlockSpec((tt, D_MODEL), lambda i, c: (i, 0)),
        scratch_shapes=[pltpu.VMEM((nrow, tt), F32), pltpu.VMEM((nrow // 8, 8, tt), F32),
                        pltpu.VMEM((nrow // 8, 8, tt), F32), pltpu.VMEM((D_MODEL, tt), F32),
                        pltpu.VMEM((ec, tt), F32), pltpu.VMEM((ec, tt), F32),
                        pltpu.VMEM((ec, tt), BF16), pltpu.VMEM((ec, tt), BF16)],
        compiler_params=_params(("parallel", "arbitrary")),
        name="peer_dense",
    )(h2, ht, s1, s2, st, u, vt, g, b)


def _prep_w_in(w_in, b_in):
    sizes = (ATT_Q_WIDTH, ATT_KV_WIDTH, ATT_KV_WIDTH, RET_QK_WIDTH, RET_QK_WIDTH, RET_V_WIDTH,
             RET_V_WIDTH, MEM_WIDTH, GATE_WIDTH)
    offs = np.cumsum(sizes)[:-1].tolist()
    wb = jnp.concatenate([w_in, b_in[None, :]], axis=0)
    aq, ak, av, rq, rk, rv, rg, mq, gates = jnp.split(wb, offs, axis=1)
    rows = wb.shape[0]
    rep = ATT_HEADS // ATT_KV_HEADS
    half = RET_QK_DIM // 2
    aq = aq.reshape(rows, ATT_KV_HEADS, rep, ATT_HEAD_DIM).transpose(0, 2, 1, 3).reshape(rows, -1)

    def halves_first(w):
        return w.reshape(rows, RET_HEADS, 2, half).transpose(0, 2, 1, 3).reshape(rows, -1)

    main = jnp.concatenate([aq, ak, av, halves_first(rq), rv, rg, mq, gates], axis=1)
    assert main.shape[1] == _W_MAIN
    rk = halves_first(rk)
    return (main[:-1].astype(BF16), main[-1:], rk[:-1].T.astype(BF16), rk[-1][:, None])


def _prep_w_branch_attn(w):
    rep = ATT_HEADS // ATT_KV_HEADS
    return (w.reshape(ATT_KV_HEADS, rep, ATT_HEAD_DIM, -1).transpose(1, 0, 2, 3)
            .reshape(ATT_Q_WIDTH, -1).astype(BF16))


def _rotation_tables(seq):
    half = RET_QK_DIM // 2
    pos = jnp.arange(seq, dtype=F32)
    theta = 1.0 / (10000.0 ** jnp.linspace(0.0, 1.0, half, dtype=F32))
    ang = pos[:, None] * theta[None, :]
    cos = jnp.tile(jnp.cos(ang), (1, LANES // half))
    sin = jnp.tile(jnp.sin(ang), (1, LANES // half))
    return cos, sin, cos.T, sin.T


def _layer(x, mem, w_in, b_in, attn_sinks, ret_gn_g, mem_ln_g, mem_ln_b, w_mem_kv,
           w_branch_attn, w_branch_ret, w_branch_mem, w_out, ln1_g, ln1_b,
           w_peer_q, peer_sub_keys, peer_u, peer_v, ln2_g, ln2_b):
    batch, seq, d = x.shape
    mlen = mem.shape[1]
    t = batch * seq
    tile = min(512, seq)
    x2 = x.reshape(t, d)
    row = lambda v: v.reshape(1, -1).astype(F32)

    w_main, b_main, w_rkt, b_rkt = _prep_w_in(w_in, b_in)
    cos, sin, cos_t, sin_t = _rotation_tables(seq)
    aq, akv, rq, rv, rg, mq, gates, rkt = _proj(
        x2, w_main, b_main, w_rkt, b_rkt, cos, sin, cos_t, sin_t, seq, tile)

    mkv = _memkv(mem.reshape(batch * mlen, d), row(mem_ln_g), row(mem_ln_b),
                 w_mem_kv.astype(BF16), batch, mlen)
    br_a, br_r, br_m = _mixers(aq, akv, attn_sinks.astype(F32), rq, rkt, rv, rg, row(ret_gn_g),
                               mq, mkv, batch, seq, mlen, tile)

    h2 = _merge(br_a, br_r, br_m, gates, x2,
                _prep_w_branch_attn(w_branch_attn), w_branch_ret.astype(BF16),
                w_branch_mem.astype(BF16), w_out.astype(BF16), row(ln1_g), row(ln1_b), tile)

    wq_t = w_peer_q.T.astype(BF16)
    keys = peer_sub_keys.reshape(2 * PEER_HEADS, PEER_N_KEYS, PEER_HALF).astype(BF16)
    ht, s1, s2, st = _peer_select(h2, wq_t, keys, tile)
    y2 = _peer_dense(h2, ht, s1, s2, st, (peer_u * (2.0 ** -0.5)).astype(BF16), peer_v.T.astype(BF16),
                     row(ln2_g), row(ln2_b), tile, 8 * PEER_N_KEYS)
    return y2.reshape(batch, seq, d)


def kernel(x, mem, w_in, b_in, attn_sinks, ret_gn_g, mem_ln_g, mem_ln_b, w_mem_kv, w_branch_attn,
           w_branch_ret, w_branch_mem, w_out, ln1_g, ln1_b, w_peer_q, peer_sub_keys, peer_u, peer_v,
           ln2_g, ln2_b):
    for l in range(DEPTH):
        x = _layer(x, mem, w_in[l], b_in[l], attn_sinks[l], ret_gn_g[l], mem_ln_g[l], mem_ln_b[l],
                   w_mem_kv[l], w_branch_attn[l], w_branch_ret[l], w_branch_mem[l], w_out[l],
                   ln1_g[l], ln1_b[l], w_peer_q[l], peer_sub_keys[l], peer_u[l], peer_v[l],
                   ln2_g[l], ln2_b[l])
    return x
```
